```python
import math
import jax, jax.numpy as jnp
from jax import lax
import numpy as np

D_MODEL = 2048
BATCH = 1
SEQ = 8192
DEPTH = 2
DEC_BATCH = 4
DEC_SEQ = 8192
PAST_LEN = 128

A_GROUPS = ((128, 1), (512, 4), (2048, 16))
A_HEADS_PER_GROUP = 4
A_HEAD_DIM = 64
A_N_HEADS = 12
A_OUT = A_HEADS_PER_GROUP * A_HEAD_DIM
B_HEADS = 6
B_QK_DIM = 64
B_V_DIM = 2 * B_QK_DIM
B_OUT = B_HEADS * B_V_DIM
B_QBLOCK = 128
C_Q_HEADS = 8
C_KV_HEADS = 2
C_HEAD_DIM = 128
C_HALF_WINDOW = 128
C_OUT = C_Q_HEADS * C_HEAD_DIM
A_COLS = 3 * A_N_HEADS * A_HEAD_DIM
B_COLS = 2 * B_HEADS * 2 * B_QK_DIM + B_HEADS * B_V_DIM
C_COLS = C_Q_HEADS * C_HEAD_DIM + 2 * C_KV_HEADS * C_HEAD_DIM
IN_COLS = A_COLS + B_COLS + C_COLS
N_BRANCH = 3
N_EXPERTS = 16
EC_CAPACITY_FACTOR = 2
D_EXPERT = D_MODEL
N_ALIBI_HEADS = C_Q_HEADS + A_N_HEADS + B_HEADS
RMS_EPS = 1e-6
NEG_INF = -1e30

kernel_name = "hybrid_bidir_encoder_ec_moe"


def rms_norm(x, g):
    xf = x.astype(jnp.float32)
    y = xf * lax.rsqrt(jnp.mean(xf * xf, axis=-1, keepdims=True) + RMS_EPS)
    return (y * g.astype(jnp.float32)).astype(x.dtype)


def alibi_slopes():
    n = N_ALIBI_HEADS
    s = 2.0 ** (-8.0 * np.arange(1, n + 1, dtype=np.float32) / n)
    s = s.astype(np.float32)
    s_c = jnp.asarray(s[:C_Q_HEADS])
    s_a = jnp.asarray(s[C_Q_HEADS:C_Q_HEADS + A_N_HEADS])
    s_b = jnp.asarray(s[C_Q_HEADS + A_N_HEADS:])
    return s_a, s_b, s_c


def banded_attention(q, k, v, half, slopes, dist_scale, sink=None):
    b, L, hk, g, dh = q.shape
    nb = -(-L // half)
    pad = nb * half - L
    qb = jnp.pad(q, ((0, 0), (0, pad), (0, 0), (0, 0), (0, 0))).reshape(b, nb, half, hk, g, dh)

    def kv_blocks(a):
        ap = jnp.pad(a, ((0, 0), (half, half + pad), (0, 0), (0, 0))).reshape(b, nb + 2, half, hk, dh)
        return jnp.concatenate([ap[:, :-2], ap[:, 1:-1], ap[:, 2:]], axis=2)

    kb = kv_blocks(k)
    vb = kv_blocks(v)
    blk = jnp.arange(nb)[:, None]
    q_pos = blk * half + jnp.arange(half)[None, :]
    k_pos = (blk - 1) * half + jnp.arange(3 * half)[None, :]
    rel = jnp.abs(k_pos[:, None, :] - q_pos[:, :, None])
    valid = (rel <= half) & (k_pos[:, None, :] >= 0) & (k_pos[:, None, :] < L)
    penalty = (slopes.astype(jnp.float32) * dist_scale)[None, None, :, :, None, None] * rel.astype(jnp.float32)[None, :, None, None, :, :]
    scores = jnp.einsum('bnqhgd,bnkhd->bnhgqk', qb, kb).astype(jnp.float32) * (dh ** -0.5) - penalty
    scores = jnp.where(valid[None, :, None, None], scores, NEG_INF)
    m = jnp.max(scores, axis=-1)
    if sink is not None:
        sink_f = sink.astype(jnp.float32)[None, None, :, :, None]
        m = jnp.maximum(m, sink_f)
    p = jnp.exp(scores - m[..., None])
    denom = jnp.sum(p, axis=-1)
    if sink is not None:
        denom = denom + jnp.exp(sink_f - m)
    out = jnp.einsum('bnhgqk,bnkhd->bnqhgd', p.astype(v.dtype), vb).astype(jnp.float32)
    out = out / jnp.transpose(denom, (0, 1, 4, 2, 3))[..., None]
    out = out.reshape(b, nb * half, hk, g, dh)[:, :L].astype(q.dtype)
    lse = jnp.transpose(m + jnp.log(denom), (0, 1, 4, 2, 3)).reshape(b, nb * half, hk, g)[:, :L]
    return out, lse


def to_dilated(a, d):
    b, L = a.shape[:2]
    rest = a.shape[2:]
    a = a.reshape((b, L // d, d) + rest)
    return jnp.moveaxis(a, 2, 1).reshape((b * d, L // d) + rest)


def from_dilated(a, d, b):
    ld = a.shape[1]
    rest = a.shape[2:]
    a = a.reshape((b, d, ld) + rest)
    return jnp.moveaxis(a, 1, 2).reshape((b, ld * d) + rest)


def dilated_mixture_attention(q, k, v, slopes):
    b, s = q.shape[:2]
    ng = len(A_GROUPS)
    q = q.reshape(b, s, ng, A_HEADS_PER_GROUP, A_HEAD_DIM)
    k = k.reshape(b, s, ng, A_HEADS_PER_GROUP, A_HEAD_DIM)
    v = v.reshape(b, s, ng, A_HEADS_PER_GROUP, A_HEAD_DIM)
    slopes = slopes.reshape(ng, A_HEADS_PER_GROUP)
    outs = []
    lses = []
    for gi, (window, dil) in enumerate(A_GROUPS):
        half = window // (2 * dil)
        qd = to_dilated(q[:, :, gi], dil)[..., None, :]
        kd = to_dilated(k[:, :, gi], dil)
        vd = to_dilated(v[:, :, gi], dil)
        o, l = banded_attention(qd, kd, vd, half, slopes[gi][:, None], dil)
        outs.append(from_dilated(o[..., 0, :], dil, b))
        lses.append(from_dilated(l[..., 0], dil, b))
    outs = jnp.stack(outs)
    lses = jnp.stack(lses)
    w = jax.nn.softmax(lses, axis=0)
    o = jnp.sum(w[..., None] * outs.astype(jnp.float32), axis=0)
    return o.reshape(b, s, A_OUT).astype(q.dtype)


def differential_attention(q, k, v, slopes, lam, lam_init, g_subln):
    b, s = q.shape[:2]
    nqb = s // B_QBLOCK
    qb = jnp.moveaxis(q.reshape(b, nqb, B_QBLOCK, B_HEADS, 2, B_QK_DIM), 1, 0)
    key_pos = jnp.arange(s)
    slopes_f = slopes.astype(jnp.float32)

    def one_block(args):
        qi, bi = args
        q_pos = bi * B_QBLOCK + jnp.arange(B_QBLOCK)
        dist = jnp.abs(q_pos[:, None] - key_pos[None, :]).astype(jnp.float32)
        sc = jnp.einsum('bqhcd,bkhcd->bhcqk', qi, k).astype(jnp.float32) * (B_QK_DIM ** -0.5)
        sc = sc - slopes_f[None, :, None, None, None] * dist
        a = jax.nn.softmax(sc, axis=-1)
        attn = a[:, :, 0] - lam * a[:, :, 1]
        return jnp.einsum('bhqk,bkhd->bqhd', attn.astype(v.dtype), v)

    o = lax.map(one_block, (qb, jnp.arange(nqb)))
    o = jnp.moveaxis(o, 0, 1).reshape(b, s, B_HEADS, B_V_DIM)
    o = rms_norm(o, g_subln) * (1.0 - lam_init)
    return o.reshape(b, s, B_OUT)


def windowed_gqa_sink(q, k, v, slopes, sink):
    b, s = q.shape[:2]
    grp = C_Q_HEADS // C_KV_HEADS
    qg = q.reshape(b, s, C_KV_HEADS, grp, C_HEAD_DIM)
    o, _ = banded_attention(qg, k, v, C_HALF_WINDOW, slopes.reshape(C_KV_HEADS, grp), 1,
                            sink.reshape(C_KV_HEADS, grp))
    return o.reshape(b, s, C_OUT)


def token_mixer(h, l, w_in, w_branch_gate, w_br_a, w_br_b, w_br_c, w_o,
                lambda_q1, lambda_k1, lambda_q2, lambda_k2, g_subln, sink):
    b, s, _ = h.shape
    proj = h @ w_in[l]
    pa, pb, pc = jnp.split(proj, [A_COLS, A_COLS + B_COLS], axis=-1)
    qa, ka, va = jnp.split(pa, 3, axis=-1)
    qa = qa.reshape(b, s, A_N_HEADS, A_HEAD_DIM)
    ka = ka.reshape(b, s, A_N_HEADS, A_HEAD_DIM)
    va = va.reshape(b, s, A_N_HEADS, A_HEAD_DIM)
    nqk = B_HEADS * 2 * B_QK_DIM
    qb, kb, vb = jnp.split(pb, [nqk, 2 * nqk], axis=-1)
    qb = qb.reshape(b, s, B_HEADS, 2, B_QK_DIM)
    kb = kb.reshape(b, s, B_HEADS, 2, B_QK_DIM)
    vb = vb.reshape(b, s, B_HEADS, B_V_DIM)
    nq = C_Q_HEADS * C_HEAD_DIM
    nkv = C_KV_HEADS * C_HEAD_DIM
    qc, kc, vc = jnp.split(pc, [nq, nq + nkv], axis=-1)
    qc = qc.reshape(b, s, C_Q_HEADS, C_HEAD_DIM)
    kc = kc.reshape(b, s, C_KV_HEADS, C_HEAD_DIM)
    vc = vc.reshape(b, s, C_KV_HEADS, C_HEAD_DIM)

    s_a, s_b, s_c = alibi_slopes()
    lam_init = 0.8 - 0.6 * math.exp(-0.3 * l)
    lam = (jnp.exp(jnp.sum(lambda_q1[l].astype(jnp.float32) * lambda_k1[l].astype(jnp.float32)))
           - jnp.exp(jnp.sum(lambda_q2[l].astype(jnp.float32) * lambda_k2[l].astype(jnp.float32)))
           + lam_init)

    oa = dilated_mixture_attention(qa, ka, va, s_a)
    ob = differential_attention(qb, kb, vb, s_b, lam, lam_init, g_subln[l])
    oc = windowed_gqa_sink(qc, kc, vc, s_c, sink[l])

    gates = jax.nn.sigmoid(h @ w_branch_gate[l])
    ga, gb, gc = jnp.split(gates, N_BRANCH, axis=-1)
    merged = ga * (oa @ w_br_a[l]) + gb * (ob @ w_br_b[l]) + gc * (oc @ w_br_c[l])
    return merged @ w_o[l]


def expert_choice_ffn(h, w_router, w_e_gate, w_e_up, w_e_down):
    b, s, d = h.shape
    n = b * s
    cap = (EC_CAPACITY_FACTOR * n) // N_EXPERTS
    xf = h.reshape(n, d)
    aff = jax.nn.softmax((xf @ w_router).astype(jnp.float32), axis=-1)
    gate, idx = lax.top_k(aff.T, cap)
    xe = xf[idx]
    hid = jax.nn.silu(jnp.einsum('ecd,edf->ecf', xe, w_e_gate)) * jnp.einsum('ecd,edf->ecf', xe, w_e_up)
    ye = jnp.einsum('ecf,efd->ecd', hid, w_e_down) * gate[..., None].astype(h.dtype)
    y = jnp.zeros((n, d), h.dtype).at[idx.reshape(-1)].add(ye.reshape(-1, d))
    return y.reshape(b, s, d)


def run_trunk(x, c, g_norm1, g_norm2, w_ada, b_ada, w_in, w_branch_gate, w_br_a, w_br_b, w_br_c, w_o,
              lambda_q1, lambda_k1, lambda_q2, lambda_k2, g_subln, sink,
              w_router, w_e_gate, w_e_up, w_e_down, g_final):
    for l in range(DEPTH):
        mod = (jax.nn.silu(c) @ w_ada[l] + b_ada[l])[:, None, :]
        sh1, sc1, gt1, sh2, sc2, gt2 = jnp.split(mod, 6, axis=-1)
        h = rms_norm(x, g_norm1[l]) * (1 + sc1) + sh1
        x = x + gt1 * token_mixer(h, l, w_in, w_branch_gate, w_br_a, w_br_b, w_br_c, w_o,
                                  lambda_q1, lambda_k1, lambda_q2, lambda_k2, g_subln, sink)
        h = rms_norm(x, g_norm2[l]) * (1 + sc2) + sh2
        x = x + gt2 * expert_choice_ffn(h, w_router[l], w_e_gate[l], w_e_up[l], w_e_down[l])
    return rms_norm(x, g_final)


def setup_inputs(seed: int = 0) -> dict:
    key = jax.random.key(seed)
    ks = jax.random.split(key, 26)

    def nrm(k, shape, scale):
        return jax.random.normal(k, shape, jnp.float32) * scale

    D = D_MODEL
    return {
        'x_prompt': nrm(ks[0], (BATCH, SEQ, D), 1.0),
        'x_sample': nrm(ks[1], (DEC_BATCH, DEC_SEQ, D), 1.0),
        'c_prompt': nrm(ks[2], (BATCH, D), 1.0),
        'c_sample': nrm(ks[3], (DEC_BATCH, D), 1.0),
        'g_norm1': 1.0 + nrm(ks[4], (DEPTH, D), 0.02),
        'g_norm2': 1.0 + nrm(ks[5], (DEPTH, D), 0.02),
        'w_ada': nrm(ks[6], (DEPTH, D, 6 * D), 0.5 * D ** -0.5),
        'b_ada': nrm(ks[7], (DEPTH, 6 * D), 0.01),
        'w_in': nrm(ks[8], (DEPTH, D, IN_COLS), D ** -0.5),
        'w_branch_gate': nrm(ks[9], (DEPTH, D, N_BRANCH * D), D ** -0.5),
        'w_br_a': nrm(ks[10], (DEPTH, A_OUT, D), A_OUT ** -0.5),
        'w_br_b': nrm(ks[11], (DEPTH, B_OUT, D), B_OUT ** -0.5),
        'w_br_c': nrm(ks[12], (DEPTH, C_OUT, D), C_OUT ** -0.5),
        'w_o': nrm(ks[13], (DEPTH, D, D), D ** -0.5),
        'lambda_q1': nrm(ks[14], (DEPTH, B_QK_DIM), 0.1),
        'lambda_k1': nrm(ks[15], (DEPTH, B_QK_DIM), 0.1),
        'lambda_q2': nrm(ks[16], (DEPTH, B_QK_DIM), 0.1),
        'lambda_k2': nrm(ks[17], (DEPTH, B_QK_DIM), 0.1),
        'g_subln': 1.0 + nrm(ks[18], (DEPTH, B_V_DIM), 0.02),
        'sink': nrm(ks[19], (DEPTH, C_Q_HEADS), 0.5),
        'w_router': nrm(ks[20], (DEPTH, D, N_EXPERTS), D ** -0.5),
        'w_e_gate': nrm(ks[21], (DEPTH, N_EXPERTS, D, D_EXPERT), D ** -0.5),
        'w_e_up': nrm(ks[22], (DEPTH, N_EXPERTS, D, D_EXPERT), D ** -0.5),
        'w_e_down': nrm(ks[23], (DEPTH, N_EXPERTS, D_EXPERT, D), D_EXPERT ** -0.5),
        'g_final': 1.0 + nrm(ks[24], (D,), 0.02),
    }


def reference(x_prompt, x_sample, c_prompt, c_sample, g_norm1, g_norm2, w_ada, b_ada, w_in, w_branch_gate,
              w_br_a, w_br_b, w_br_c, w_o, lambda_q1, lambda_k1, lambda_q2, lambda_k2, g_subln, sink,
              w_router, w_e_gate, w_e_up, w_e_down, g_final):
    y_prompt = run_trunk(x_prompt, c_prompt, g_norm1, g_norm2, w_ada, b_ada, w_in, w_branch_gate,
                         w_br_a, w_br_b, w_br_c, w_o, lambda_q1, lambda_k1, lambda_q2, lambda_k2,
                         g_subln, sink, w_router, w_e_gate, w_e_up, w_e_down, g_final)
    y_sample = run_trunk(x_sample, c_sample, g_norm1, g_norm2, w_ada, b_ada, w_in, w_branch_gate,
                         w_br_a, w_br_b, w_br_c, w_o, lambda_q1, lambda_k1, lambda_q2, lambda_k2,
                         g_subln, sink, w_router, w_e_gate, w_e_up, w_e_down, g_final)
    return (y_prompt, y_sample)
```

```python
import functools
import math

import numpy as np
import jax
import jax.numpy as jnp
from jax import lax
from jax.experimental import pallas as pl
from jax.experimental.pallas import tpu as pltpu

F32 = jnp.float32
BF16 = jnp.bfloat16

A_GROUPS = ((128, 1), (512, 4), (2048, 16))
A_HEADS_PER_GROUP = 4
A_HEAD_DIM = 64
A_N_HEADS = 12
A_OUT = 256
A_GROUP_COLS = A_HEADS_PER_GROUP * A_HEAD_DIM
B_HEADS = 6
B_QK_DIM = 64
B_V_DIM = 128
B_OUT = 768
C_Q_HEADS = 8
C_KV_HEADS = 2
C_GROUP = C_Q_HEADS // C_KV_HEADS
C_HEAD_DIM = 128
C_HALF_WINDOW = 128
C_OUT = 1024
A_COLS = 2304
B_COLS = 2304
C_COLS = 1536
IN_COLS = 6144
N_BRANCH = 3
N_EXPERTS = 16
EC_CAPACITY_FACTOR = 2
N_ALIBI_HEADS = 26
RMS_EPS = 1e-6
NEG_INF = -1e30
N_MOD = 6
SUBLANES = 8
LANES = 128

_NT_DIMS = (((1,), (1,)), ((), ()))
_MIB = 1024 * 1024


def _alibi_slopes():
    n = N_ALIBI_HEADS
    s = 2.0 ** (-8.0 * np.arange(1, n + 1, dtype=np.float32) / n)
    s = s.astype(np.float32)
    s_c = s[:C_Q_HEADS]
    s_a = s[C_Q_HEADS:C_Q_HEADS + A_N_HEADS]
    s_b = s[C_Q_HEADS + A_N_HEADS:]
    return s_a, s_b, s_c


def _params(semantics, vmem_mib):
    return pltpu.CompilerParams(dimension_semantics=semantics,
                                vmem_limit_bytes=vmem_mib * _MIB)


def _mod_spec(layer, chunk, seq_of):
    def index(*ids):
        return (layer, seq_of(*ids), chunk, 0, 0)
    return index


def _rms_mod(x, g, scale, shift):
    ms = jnp.mean(x * x, axis=-1, keepdims=True)
    y = x * lax.rsqrt(ms + RMS_EPS) * g
    return y * (1.0 + scale) + shift


def _ada_kernel(c_ref, w_ref, b_ref, o_ref):
    c = c_ref[...]
    a = c * jax.nn.sigmoid(c)
    o_ref[...] = jnp.dot(a, w_ref[...], precision=lax.Precision.HIGHEST,
                         preferred_element_type=F32) + b_ref[...]


def _ada_modulation(c_rows, w_ada, b_ada, tn=1024):
    depth, d, n6 = w_ada.shape
    rows = c_rows.shape[0]
    return pl.pallas_call(
        _ada_kernel,
        out_shape=jax.ShapeDtypeStruct((depth, rows, n6), F32),
        grid=(depth, n6 // tn),
        in_specs=[
            pl.BlockSpec((rows, d), lambda l, j: (0, 0)),
            pl.BlockSpec((None, d, tn), lambda l, j: (l, 0, j)),
            pl.BlockSpec((None, 1, tn), lambda l, j: (l, 0, j)),
        ],
        out_specs=pl.BlockSpec((None, rows, tn), lambda l, j: (l, 0, j)),
        compiler_params=_params(("arbitrary", "arbitrary"), 40),
        name="ada_modulation",
    )(c_rows, w_ada, b_ada.reshape(depth, 1, n6))


def _proj_kernel(x_ref, g_ref, sc_ref, sh_ref, w_ref, o_ref, h_scr, *, n_plain):
    j = pl.program_id(1)

    @pl.when(j == 0)
    def _():
        h = _rms_mod(x_ref[...], g_ref[...], sc_ref[...], sh_ref[...])
        h_scr[...] = h.astype(BF16)

    acc = jnp.dot(h_scr[...], w_ref[...], preferred_element_type=F32)

    @pl.when(j < n_plain)
    def _():
        o_ref[...] = acc.astype(BF16)

    @pl.when(j >= n_plain)
    def _():
        o_ref[...] = jax.nn.sigmoid(acc).astype(BF16)


def _norm_proj(x, g, mod, w_cat, layer, seq_len, tm=1024, tn=1024):
    n, d = x.shape
    ncols = w_cat.shape[-1]
    tm = min(tm, seq_len)
    tiles_per_seq = seq_len // tm
    seq_of = lambda i, j: i // tiles_per_seq
    kern = functools.partial(_proj_kernel, n_plain=IN_COLS // tn)
    return pl.pallas_call(
        kern,
        out_shape=jax.ShapeDtypeStruct((n, ncols), BF16),
        grid=(n // tm, ncols // tn),
        in_specs=[
            pl.BlockSpec((tm, d), lambda i, j: (i, 0)),
            pl.BlockSpec((None, 1, d), lambda i, j: (layer, 0, 0)),
            pl.BlockSpec((None, None, None, 1, d), _mod_spec(layer, 1, seq_of)),
            pl.BlockSpec((None, None, None, 1, d), _mod_spec(layer, 0, seq_of)),
            pl.BlockSpec((None, d, tn), lambda i, j: (layer, 0, j)),
        ],
        out_specs=pl.BlockSpec((tm, tn), lambda i, j: (i, j)),
        scratch_shapes=[pltpu.VMEM((tm, d), BF16)],
        compiler_params=_params(("arbitrary", "arbitrary"), 48),
        name="norm_proj",
    )(x, g, mod, mod, w_cat)


def _battn_kernel(slope_ref, q_ref, k_ref, v_ref, lq1_ref, lk1_ref, lq2_ref, lk2_ref, gs_ref,
                  o_ref, *, tq, tk, seq_len, lam_init):
    h = pl.program_id(1)
    qi = pl.program_id(2)
    slope = slope_ref[h]
    q = q_ref[...] * jnp.asarray(B_QK_DIM ** -0.5, BF16)
    lane = lax.broadcasted_iota(jnp.int32, (tq, 2 * B_QK_DIM), 1)
    zero = jnp.zeros_like(q)
    qq = jnp.concatenate([jnp.where(lane < B_QK_DIM, q, zero),
                          jnp.where(lane >= B_QK_DIM, q, zero)], axis=0)
    row = lax.broadcasted_iota(jnp.int32, (2 * tq, 1), 0)
    qpos = qi * tq + jnp.where(row >= tq, row - tq, row)

    def body(j, carry):
        m, l, acc = carry
        k0 = pl.multiple_of(j * tk, tk)
        kc = k_ref[pl.ds(k0, tk), :]
        vc = v_ref[pl.ds(k0, tk), :]
        s = lax.dot_general(qq, kc, _NT_DIMS, preferred_element_type=F32)
        kpos = k0 + lax.broadcasted_iota(jnp.int32, (1, tk), 1)
        s = s - slope * jnp.abs(qpos - kpos).astype(F32)
        m_new = jnp.maximum(m, jnp.max(s, axis=-1, keepdims=True))
        alpha = jnp.exp(m - m_new)
        p = jnp.exp(s - m_new)
        l = alpha * l + jnp.sum(p, axis=-1, keepdims=True)
        acc = alpha * acc + jnp.dot(p.astype(BF16), vc, preferred_element_type=F32)
        return m_new, l, acc

    init = (jnp.full((2 * tq, 1), NEG_INF, F32), jnp.zeros((2 * tq, 1), F32),
            jnp.zeros((2 * tq, B_V_DIM), F32))
    _, l, acc = lax.fori_loop(0, seq_len // tk, body, init)
    o = acc / l
    lam = (jnp.exp(jnp.sum(lq1_ref[...] * lk1_ref[...], axis=-1, keepdims=True))
           - jnp.exp(jnp.sum(lq2_ref[...] * lk2_ref[...], axis=-1, keepdims=True)) + lam_init)
    diff = o[:tq] - lam * o[tq:]
    ms = jnp.mean(diff * diff, axis=-1, keepdims=True)
    y = diff * lax.rsqrt(ms + RMS_EPS) * gs_ref[...]
    o_ref[...] = (y * (1.0 - lam_init)).astype(BF16)


def _diff_attention(proj, slopes_b, lq1, lk1, lq2, lk2, g_subln, layer, batch, seq_len,
                    tq=256, tk=512):
    n = proj.shape[0]
    tq = min(tq, seq_len)
    tk = min(tk, seq_len)
    nq = seq_len // tq
    qcol = A_COLS // B_V_DIM
    kcol = qcol + B_HEADS
    vcol = kcol + B_HEADS
    lam_init = 0.8 - 0.6 * math.exp(-0.3 * layer)
    kern = functools.partial(_battn_kernel, tq=tq, tk=tk, seq_len=seq_len, lam_init=lam_init)
    vec = lambda width: pl.BlockSpec((None, 1, width), lambda b, h, i: (layer, 0, 0))
    return pl.pallas_call(
        kern,
        out_shape=jax.ShapeDtypeStruct((n, B_OUT), BF16),
        grid=(batch, B_HEADS, nq),
        in_specs=[
            pl.BlockSpec(memory_space=pltpu.SMEM),
            pl.BlockSpec((tq, B_V_DIM), lambda b, h, i: (b * nq + i, qcol + h)),
            pl.BlockSpec((seq_len, B_V_DIM), lambda b, h, i: (b, kcol + h)),
            pl.BlockSpec((seq_len, B_V_DIM), lambda b, h, i: (b, vcol + h)),
            vec(B_QK_DIM), vec(B_QK_DIM), vec(B_QK_DIM), vec(B_QK_DIM), vec(B_V_DIM),
        ],
        out_specs=pl.BlockSpec((tq, B_V_DIM), lambda b, h, i: (b * nq + i, h)),
        compiler_params=_params(("arbitrary", "arbitrary", "arbitrary"), 40),
        name="diff_attention",
    )(slopes_b, proj, proj, proj, lq1, lk1, lq2, lk2, g_subln)


def _band_window(qi, tq, half, length):
    kw = tq + 2 * half
    ks = jnp.clip(qi * tq - half, 0, length - kw)
    ks = pl.multiple_of(ks, half)
    qpos = qi * tq + lax.broadcasted_iota(jnp.int32, (tq, 1), 0)
    kpos = ks + lax.broadcasted_iota(jnp.int32, (1, kw), 1)
    rel = jnp.abs(qpos - kpos)
    return ks, kw, rel <= half, rel.astype(F32)


def _cattn_kernel(slope_ref, sink_ref, q_ref, k_ref, v_ref, o_ref, *, tq, seq_len):
    g = pl.program_id(1)
    qi = pl.program_id(2)
    ks, kw, valid, relf = _band_window(qi, tq, C_HALF_WINDOW, seq_len)
    kwin = k_ref[pl.ds(ks, kw), :]
    vwin = v_ref[pl.ds(ks, kw), :]
    scale = C_HEAD_DIM ** -0.5
    for hh in range(C_GROUP):
        head = g * C_GROUP + hh
        slope = slope_ref[head]
        sink = sink_ref[head]
        cols = slice(hh * C_HEAD_DIM, (hh + 1) * C_HEAD_DIM)
        s = lax.dot_general(q_ref[:, cols], kwin, _NT_DIMS, preferred_element_type=F32)
        s = jnp.where(valid, s * scale - slope * relf, NEG_INF)
        m = jnp.maximum(jnp.max(s, axis=-1, keepdims=True), sink)
        p = jnp.exp(s - m)
        den = jnp.sum(p, axis=-1, keepdims=True) + jnp.exp(sink - m)
        o = jnp.dot(p.astype(BF16), vwin, preferred_element_type=F32) / den
        o_ref[:, cols] = o.astype(BF16)


def _window_attention(proj, slopes_c, sink, batch, seq_len, tq=256):
    n = proj.shape[0]
    nq = seq_len // tq
    gw = C_GROUP * C_HEAD_DIM
    qcol = (A_COLS + B_COLS) // gw
    kcol = (A_COLS + B_COLS + C_OUT) // C_HEAD_DIM
    vcol = kcol + C_KV_HEADS
    kern = functools.partial(_cattn_kernel, tq=tq, seq_len=seq_len)
    return pl.pallas_call(
        kern,
        out_shape=jax.ShapeDtypeStruct((n, C_OUT), BF16),
        grid=(batch, C_KV_HEADS, nq),
        in_specs=[
            pl.BlockSpec(memory_space=pltpu.SMEM),
            pl.BlockSpec(memory_space=pltpu.SMEM),
            pl.BlockSpec((tq, gw), lambda b, g, i: (b * nq + i, qcol + g)),
            pl.BlockSpec((seq_len, C_HEAD_DIM), lambda b, g, i: (b, kcol + g)),
            pl.BlockSpec((seq_len, C_HEAD_DIM), lambda b, g, i: (b, vcol + g)),
        ],
        out_specs=pl.BlockSpec((tq, gw), lambda b, g, i: (b * nq + i, g)),
        compiler_params=_params(("arbitrary", "arbitrary", "arbitrary"), 32),
        name="window_attention",
    )(slopes_c, sink, proj, proj, proj)


def _aattn_kernel(q_ref, k_ref, v_ref, o_ref, lse_ref, *, tq, length, slopes, dil):
    qi = pl.program_id(2)
    half = A_GROUPS[0][0] // 2
    ks, kw, valid, relf = _band_window(qi, tq, half, length)
    kwin = k_ref[pl.ds(ks, kw), :]
    vwin = v_ref[pl.ds(ks, kw), :]
    q = q_ref[...] * jnp.asarray(A_HEAD_DIM ** -0.5, BF16)
    zero = jnp.zeros_like(q)
    lane = lax.broadcasted_iota(jnp.int32, (tq, A_GROUP_COLS), 1)
    out = jnp.zeros((tq, A_GROUP_COLS), F32)
    lse_out = jnp.zeros((tq, A_GROUP_COLS), F32)
    for hh in range(A_HEADS_PER_GROUP):
        in_head = (lane >= hh * A_HEAD_DIM) & (lane < (hh + 1) * A_HEAD_DIM)
        qh = jnp.where(in_head, q, zero)
        s = lax.dot_general(qh, kwin, _NT_DIMS, preferred_element_type=F32)
        s = jnp.where(valid, s - (slopes[hh] * dil) * relf, NEG_INF)
        m = jnp.max(s, axis=-1, keepdims=True)
        p = jnp.exp(s - m)
        den = jnp.sum(p, axis=-1, keepdims=True)
        o = jnp.dot(p.astype(BF16), vwin, preferred_element_type=F32) / den
        out = jnp.where(in_head, o, out)
        lse_out = jnp.where(in_head, m + jnp.log(den), lse_out)
    o_ref[...] = out
    lse_ref[...] = lse_out


def _dilated_attention(proj, group, slopes, batch, seq_len, tq=256):
    n, ncols = proj.shape
    window, dil = A_GROUPS[group]
    half = window // (2 * dil)
    length = seq_len // dil
    tq = min(tq, length - 2 * half)
    nq = length // tq
    cols_per_row = ncols // A_GROUP_COLS
    ngroups = len(A_GROUPS)
    view = proj.reshape(n // dil, dil * ncols)
    kern = functools.partial(_aattn_kernel, tq=tq, length=length,
                             slopes=tuple(float(v) for v in slopes), dil=float(dil))
    out_sds = jax.ShapeDtypeStruct((n // dil, dil * A_GROUP_COLS), F32)
    out_spec = pl.BlockSpec((tq, A_GROUP_COLS), lambda b, r, i: (b * nq + i, r))
    out, lse = pl.pallas_call(
        kern,
        out_shape=(out_sds, out_sds),
        grid=(batch, dil, nq),
        in_specs=[
            pl.BlockSpec((tq, A_GROUP_COLS),
                         lambda b, r, i: (b * nq + i, r * cols_per_row + group)),
            pl.BlockSpec((length, A_GROUP_COLS),
                         lambda b, r, i: (b, r * cols_per_row + ngroups + group)),
            pl.BlockSpec((length, A_GROUP_COLS),
                         lambda b, r, i: (b, r * cols_per_row + 2 * ngroups + group)),
        ],
        out_specs=(out_spec, out_spec),
        compiler_params=_params(("arbitrary", "arbitrary", "arbitrary"), 32),
        name=f"dilated_attention_g{group}",
    )(view, view, view)
    return out.reshape(n, A_GROUP_COLS), lse.reshape(n, A_GROUP_COLS)


def _merge_kernel(x_ref, gt_ref, o0_ref, o1_ref, o2_ref, l0_ref, l1_ref, l2_ref, ob_ref, oc_ref,
                  ga_ref, gb_ref, gc_ref, wa_ref, wb_ref, wc_ref, wo_ref, out_ref):
    l0, l1, l2 = l0_ref[...], l1_ref[...], l2_ref[...]
    m = jnp.maximum(jnp.maximum(l0, l1), l2)
    e0, e1, e2 = jnp.exp(l0 - m), jnp.exp(l1 - m), jnp.exp(l2 - m)
    oa = (e0 * o0_ref[...] + e1 * o1_ref[...] + e2 * o2_ref[...]) / (e0 + e1 + e2)
    ya = jnp.dot(oa.astype(BF16), wa_ref[...], preferred_element_type=F32)
    yb = jnp.dot(ob_ref[...], wb_ref[...], preferred_element_type=F32)
    yc = jnp.dot(oc_ref[...], wc_ref[...], preferred_element_type=F32)
    merged = (ga_ref[...].astype(F32) * ya + gb_ref[...].astype(F32) * yb
              + gc_ref[...].astype(F32) * yc)
    y = jnp.dot(merged.astype(BF16), wo_ref[...], preferred_element_type=F32)
    out_ref[...] = x_ref[...] + gt_ref[...] * y


def _merge(x, mod, oa_parts, ob, oc, proj, w_br_a, w_br_b, w_br_c, w_o, layer, seq_len, tm=256):
    n, d = x.shape
    tm = min(tm, seq_len)
    tiles_per_seq = seq_len // tm
    seq_of = lambda i: i // tiles_per_seq
    (o0, l0), (o1, l1), (o2, l2) = oa_parts
    gate_col = IN_COLS // d
    row = lambda width: pl.BlockSpec((tm, width), lambda i: (i, 0))
    resident = lambda rows: pl.BlockSpec((None, rows, d), lambda i: (layer, 0, 0),
                                         pipeline_mode=pl.Buffered(1))
    gate = lambda k: pl.BlockSpec((tm, d), lambda i: (i, gate_col + k))
    return pl.pallas_call(
        _merge_kernel,
        out_shape=jax.ShapeDtypeStruct((n, d), F32),
        grid=(n // tm,),
        in_specs=[
            row(d),
            pl.BlockSpec((None, None, None, 1, d), _mod_spec(layer, 2, seq_of)),
            row(A_OUT), row(A_OUT), row(A_OUT), row(A_OUT), row(A_OUT), row(A_OUT),
            row(B_OUT), row(C_OUT),
            gate(0), gate(1), gate(2),
            resident(A_OUT), resident(B_OUT), resident(C_OUT), resident(d),
        ],
        out_specs=row(d),
        compiler_params=_params(("arbitrary",), 52),
        name="merge",
    )(x, mod, o0, o1, o2, l0, l1, l2, ob, oc, proj, proj, proj, w_br_a, w_br_b, w_br_c, w_o)


def _router_kernel(x_ref, g_ref, sc_ref, sh_ref, wr_ref, h_ref, aff_ref):
    h = _rms_mod(x_ref[...], g_ref[...], sc_ref[...], sh_ref[...])
    h_ref[...] = h.astype(BF16)
    logits = lax.dot_general(wr_ref[...], h, _NT_DIMS, precision=lax.Precision.HIGHEST,
                             preferred_element_type=F32)
    m = jnp.max(logits, axis=0, keepdims=True)
    e = jnp.exp(logits - m)
    aff_ref[...] = e / jnp.sum(e, axis=0, keepdims=True)


def _norm_router(x, g, mod, w_router_t, layer, seq_len, tm=512):
    n, d = x.shape
    tm = min(tm, seq_len)
    tiles_per_seq = seq_len // tm
    seq_of = lambda i: i // tiles_per_seq
    ne = w_router_t.shape[1]
    return pl.pallas_call(
        _router_kernel,
        out_shape=(jax.ShapeDtypeStruct((n, d), BF16), jax.ShapeDtypeStruct((ne, n), F32)),
        grid=(n // tm,),
        in_specs=[
            pl.BlockSpec((tm, d), lambda i: (i, 0)),
            pl.BlockSpec((None, 1, d), lambda i: (layer, 0, 0)),
            pl.BlockSpec((None, None, None, 1, d), _mod_spec(layer, 4, seq_of)),
            pl.BlockSpec((None, None, None, 1, d), _mod_spec(layer, 3, seq_of)),
            pl.BlockSpec((None, ne, d), lambda i: (layer, 0, 0)),
        ],
        out_specs=(pl.BlockSpec((tm, d), lambda i: (i, 0)),
                   pl.BlockSpec((ne, tm), lambda i: (0, i))),
        compiler_params=_params(("arbitrary",), 32),
        name="norm_router",
    )(x, g, mod, mod, w_router_t)


def _ffn_kernel(x_ref, gate_ref, wg_ref, wu_ref, wd_ref, o_ref):
    f = pl.program_id(2)
    x = x_ref[...]
    a = jnp.dot(x, wg_ref[...], preferred_element_type=F32)
    u = jnp.dot(x, wu_ref[...], preferred_element_type=F32)
    hid = (a * jax.nn.sigmoid(a) * u).astype(BF16)
    part = jnp.dot(hid, wd_ref[...], preferred_element_type=F32)

    @pl.when(f == 0)
    def _():
        o_ref[...] = part

    @pl.when(f > 0)
    def _():
        o_ref[...] += part

    @pl.when(f == pl.num_programs(2) - 1)
    def _():
        o_ref[...] = o_ref[...] * gate_ref[...]


def _expert_ffn(xe, gate, w_gate, w_up, w_down, layer, tm=1024, tf=512):
    ne, cap, d = xe.shape
    dff = w_gate.shape[-1]
    tm = min(tm, cap)
    return pl.pallas_call(
        _ffn_kernel,
        out_shape=jax.ShapeDtypeStruct((ne, cap, d), F32),
        grid=(ne, cap // tm, dff // tf),
        in_specs=[
            pl.BlockSpec((None, tm, d), lambda e, i, f: (e, i, 0)),
            pl.BlockSpec((None, tm, 1), lambda e, i, f: (e, i, 0)),
            pl.BlockSpec((None, None, d, tf), lambda e, i, f: (layer, e, 0, f)),
            pl.BlockSpec((None, None, d, tf), lambda e, i, f: (layer, e, 0, f)),
            pl.BlockSpec((None, None, tf, d), lambda e, i, f: (layer, e, f, 0)),
        ],
        out_specs=pl.BlockSpec((None, tm, d), lambda e, i, f: (e, i, 0)),
        compiler_params=_params(("arbitrary", "arbitrary", "arbitrary"), 48),
        name="expert_ffn",
    )(xe, gate, w_gate, w_up, w_down)


def _residual_kernel(x_ref, y_ref, gt_ref, gf_ref, o_ref, *, final_norm):
    x = x_ref[...] + gt_ref[...] * y_ref[...]
    if final_norm:
        ms = jnp.mean(x * x, axis=-1, keepdims=True)
        x = x * lax.rsqrt(ms + RMS_EPS) * gf_ref[...]
    o_ref[...] = x


def _residual(x, y, mod, g_final, layer, seq_len, final_norm, tm=512):
    n, d = x.shape
    tm = min(tm, seq_len)
    tiles_per_seq = seq_len // tm
    seq_of = lambda i: i // tiles_per_seq
    kern = functools.partial(_residual_kernel, final_norm=final_norm)
    row = pl.BlockSpec((tm, d), lambda i: (i, 0))
    return pl.pallas_call(
        kern,
        out_shape=jax.ShapeDtypeStruct((n, d), F32),
        grid=(n // tm,),
        in_specs=[row, row,
                  pl.BlockSpec((None, None, None, 1, d), _mod_spec(layer, 5, seq_of)),
                  pl.BlockSpec((1, d), lambda i: (0, 0))],
        out_specs=row,
        compiler_params=_params(("arbitrary",), 40),
        name="residual",
    )(x, y, mod, g_final)


def _trunk(x, c_rows, group_sizes, seq_len, g_norm1, g_norm2, w_ada, b_ada, w_in, w_branch_gate,
           w_br_a, w_br_b, w_br_c, w_o, lambda_q1, lambda_k1, lambda_q2, lambda_k2, g_subln, sink,
           w_router, w_e_gate, w_e_up, w_e_down, g_final):
    n, d = x.shape
    depth = w_in.shape[0]
    batch = n // seq_len
    s_a, s_b, s_c = _alibi_slopes()

    w_cat = jnp.concatenate([w_in, w_branch_gate], axis=-1).astype(BF16)
    wa, wb, wc, wo = (w.astype(BF16) for w in (w_br_a, w_br_b, w_br_c, w_o))
    weg, weu, wed = (w.astype(BF16) for w in (w_e_gate, w_e_up, w_e_down))
    w_router_t = jnp.swapaxes(w_router, 1, 2)
    row3 = lambda a: a.reshape(depth, 1, a.shape[-1])
    g1, g2 = row3(g_norm1), row3(g_norm2)
    lq1, lk1, lq2, lk2, gsub = (row3(a) for a in (lambda_q1, lambda_k1, lambda_q2, lambda_k2, g_subln))

    mod = _ada_modulation(c_rows, w_ada, b_ada)
    mod = mod.reshape(depth, c_rows.shape[0], N_MOD, 1, d)

    for layer in range(depth):
        proj = _norm_proj(x, g1, mod, w_cat, layer, seq_len)
        oa_parts = [
            _dilated_attention(proj, gi, s_a[gi * A_HEADS_PER_GROUP:(gi + 1) * A_HEADS_PER_GROUP],
                               batch, seq_len)
            for gi in range(len(A_GROUPS))
        ]
        ob = _diff_attention(proj, jnp.asarray(s_b), lq1, lk1, lq2, lk2, gsub, layer, batch, seq_len)
        oc = _window_attention(proj, jnp.asarray(s_c), sink[layer], batch, seq_len)
        x = _merge(x, mod, oa_parts, ob, oc, proj, wa, wb, wc, wo, layer, seq_len)

        h2, aff_t = _norm_router(x, g2, mod, w_router_t, layer, seq_len)
        idx_parts, gate_parts = [], []
        start = 0
        for size in group_sizes:
            cap = (EC_CAPACITY_FACTOR * size) // N_EXPERTS
            gate, idx = lax.top_k(aff_t[:, start:start + size], cap)
            idx_parts.append(idx + start)
            gate_parts.append(gate)
            start += size
        idx = jnp.concatenate(idx_parts, axis=1)
        gate = jnp.concatenate(gate_parts, axis=1)
        xe = jnp.take(h2, idx, axis=0)
        ye = _expert_ffn(xe, gate[..., None], weg, weu, wed, layer)
        y = jnp.zeros((n, d), F32).at[idx.reshape(-1)].add(ye.reshape(-1, d))
        x = _residual(x, y, mod, g_final.reshape(1, d), layer, seq_len, layer == depth - 1)
    return x


def kernel(x_prompt, x_sample, c_prompt, c_sample, g_norm1, g_norm2, w_ada, b_ada, w_in, w_branch_gate, w_br_a, w_br_b, w_br_c, w_o, lambda_q1, lambda_k1, lambda_q2, lambda_k2, g_subln, sink, w_router, w_e_gate, w_e_up, w_e_down, g_final):
    bp, seq_len, d = x_prompt.shape
    bs = x_sample.shape[0]
    assert x_sample.shape[1] == seq_len
    x = jnp.concatenate([x_prompt.reshape(bp * seq_len, d), x_sample.reshape(bs * seq_len, d)], axis=0)
    c = jnp.concatenate([c_prompt, c_sample], axis=0)
    pad = -c.shape[0] % SUBLANES
    c_rows = jnp.pad(c, ((0, pad), (0, 0)))
    y = _trunk(x, c_rows, (bp * seq_len, bs * seq_len), seq_len, g_norm1, g_norm2, w_ada, b_ada,
               w_in, w_branch_gate, w_br_a, w_br_b, w_br_c, w_o, lambda_q1, lambda_k1, lambda_q2,
               lambda_k2, g_subln, sink, w_router, w_e_gate, w_e_up, w_e_down, g_final)
    y_prompt = y[:bp * seq_len].reshape(bp, seq_len, d)
    y_sample = y[bp * seq_len:].reshape(bs, seq_len, d)
    return (y_prompt, y_sample)
```

```python
import functools
import math

import numpy as np
import jax
import jax.numpy as jnp
from jax import lax
from jax.experimental import pallas as pl
from jax.experimental.pallas import tpu as pltpu

F32 = jnp.float32
BF16 = jnp.bfloat16

A_GROUPS = ((128, 1), (512, 4), (2048, 16))
A_HEADS_PER_GROUP = 4
A_HEAD_DIM = 64
A_N_HEADS = 12
A_OUT = 256
A_GROUP_COLS = A_HEADS_PER_GROUP * A_HEAD_DIM
A_TILE = 3 * A_GROUP_COLS
B_HEADS = 6
B_QK_DIM = 64
B_V_DIM = 128
B_OUT = 768
C_Q_HEADS = 8
C_KV_HEADS = 2
C_GROUP = C_Q_HEADS // C_KV_HEADS
C_HEAD_DIM = 128
C_HALF_WINDOW = 128
C_OUT = 1024
A_COLS = 2304
B_COLS = 2304
C_COLS = 1536
IN_COLS = 6144
N_BRANCH = 3
N_EXPERTS = 16
EC_CAPACITY_FACTOR = 2
N_ALIBI_HEADS = 26
RMS_EPS = 1e-6
NEG_INF = -1e30
LOG2E = math.log2(math.e)
N_MOD = 6
SUBLANES = 8
LANES = 128
POS_SPLIT = 64
N_PIECES = 3

_NT_DIMS = (((1,), (1,)), ((), ()))
_MIB = 1024 * 1024


def _alibi_slopes():
    n = N_ALIBI_HEADS
    s = 2.0 ** (-8.0 * np.arange(1, n + 1, dtype=np.float32) / n)
    s = s.astype(np.float32)
    s_c = s[:C_Q_HEADS]
    s_a = s[C_Q_HEADS:C_Q_HEADS + A_N_HEADS]
    s_b = s[C_Q_HEADS + A_N_HEADS:]
    return s_a, s_b, s_c


def _params(semantics, vmem_mib):
    return pltpu.CompilerParams(dimension_semantics=semantics,
                                vmem_limit_bytes=vmem_mib * _MIB)


def _mod_spec(layer, chunk, seq_of):
    def index(*ids):
        return (layer, seq_of(*ids), chunk, 0, 0)
    return index


def _rms_mod(x, g, scale, shift):
    ms = jnp.mean(x * x, axis=-1, keepdims=True)
    y = x * lax.rsqrt(ms + RMS_EPS) * g
    return y * (1.0 + scale) + shift


def _split3(x):
    hi = x.astype(BF16).astype(F32)
    mid = (x - hi).astype(BF16).astype(F32)
    lo = (x - hi - mid).astype(BF16).astype(F32)
    return hi, mid, lo


def _ada_kernel(c_ref, w_ref, b_ref, o_ref):
    c = c_ref[...]
    a = c * jax.nn.sigmoid(c)
    o_ref[...] = jnp.dot(a, w_ref[...], precision=lax.Precision.HIGHEST,
                         preferred_element_type=F32) + b_ref[...]


def _ada_modulation(c_rows, w_ada, b_ada, tn=1024):
    depth, d, n6 = w_ada.shape
    rows = c_rows.shape[0]
    return pl.pallas_call(
        _ada_kernel,
        out_shape=jax.ShapeDtypeStruct((depth, rows, n6), F32),
        grid=(depth, n6 // tn),
        in_specs=[
            pl.BlockSpec((rows, d), lambda l, j: (0, 0)),
            pl.BlockSpec((None, d, tn), lambda l, j: (l, 0, j)),
            pl.BlockSpec((None, 1, tn), lambda l, j: (l, 0, j)),
        ],
        out_specs=pl.BlockSpec((None, rows, tn), lambda l, j: (l, 0, j)),
        compiler_params=_params(("arbitrary", "arbitrary"), 40),
        name="ada_modulation",
    )(c_rows, w_ada, b_ada.reshape(depth, 1, n6))


PROJ_TILE = A_TILE
N_A_TILES = len(A_GROUPS)
N_MIX_TILES = (C_COLS + B_COLS) // PROJ_TILE
N_GATE_TILES = IN_COLS // PROJ_TILE


def _proj_kernel(x_ref, g_ref, sc_ref, sh_ref, w_ref, cs_ref, a0_ref, a1_ref, a2_ref, mix_ref,
                 gate_ref, h_scr, acc_scr, *, tm):
    j = pl.program_id(1)

    @pl.when(j == 0)
    def _():
        h = _rms_mod(x_ref[...], g_ref[...], sc_ref[...], sh_ref[...])
        h_scr[...] = h.astype(BF16)

    acc = jnp.dot(h_scr[...], w_ref[...], preferred_element_type=F32) * cs_ref[...]

    @pl.when(j == 0)
    def _():
        a0_ref[0] = acc.astype(BF16)

    for gi, a_ref in ((1, a1_ref), (2, a2_ref)):
        dil = A_GROUPS[gi][1]

        @pl.when(j == gi)
        def _(a_ref=a_ref, dil=dil):
            nslab = PROJ_TILE // LANES
            for c in range(nslab):
                acc_scr[c] = acc[:, c * LANES:(c + 1) * LANES]
            for r in range(dil):
                rows = [acc_scr[c, pl.ds(r, tm // dil, stride=dil), :] for c in range(nslab)]
                a_ref[r] = jnp.concatenate(rows, axis=1).astype(BF16)

    @pl.when((j >= N_A_TILES) & (j < N_A_TILES + N_MIX_TILES))
    def _():
        mix_ref[...] = acc.astype(BF16)

    @pl.when(j >= N_A_TILES + N_MIX_TILES)
    def _():
        gate_ref[...] = jax.nn.sigmoid(acc).astype(BF16)


def _norm_proj(x, g, mod, w_cat, col_scale, layer, batch, seq_len, tm=1024):
    n, d = x.shape
    tn = PROJ_TILE
    tm = min(tm, seq_len)
    tps = seq_len // tm
    seq_of = lambda i, j: i // tps
    a_shapes, a_specs = [], []
    for _, dil in A_GROUPS:
        a_shapes.append(jax.ShapeDtypeStruct((batch, dil, seq_len // dil, tn), BF16))
        a_specs.append(pl.BlockSpec((None, dil, tm // dil, tn), lambda i, j: (i // tps, 0, i % tps, 0)))
    first_mix, first_gate = N_A_TILES, N_A_TILES + N_MIX_TILES
    kern = functools.partial(_proj_kernel, tm=tm)
    return pl.pallas_call(
        kern,
        out_shape=(*a_shapes,
                   jax.ShapeDtypeStruct((n, N_MIX_TILES * tn), BF16),
                   jax.ShapeDtypeStruct((n, N_GATE_TILES * tn), BF16)),
        grid=(n // tm, first_gate + N_GATE_TILES),
        in_specs=[
            pl.BlockSpec((tm, d), lambda i, j: (i, 0), pipeline_mode=pl.Buffered(1)),
            pl.BlockSpec((None, 1, d), lambda i, j: (layer, 0, 0)),
            pl.BlockSpec((None, None, None, 1, d), _mod_spec(layer, 1, seq_of)),
            pl.BlockSpec((None, None, None, 1, d), _mod_spec(layer, 0, seq_of)),
            pl.BlockSpec((None, d, tn), lambda i, j: (layer, 0, j)),
            pl.BlockSpec((1, tn), lambda i, j: (0, j)),
        ],
        out_specs=(*a_specs,
                   pl.BlockSpec((tm, tn), lambda i, j: (i, jnp.clip(j - first_mix, 0, N_MIX_TILES - 1))),
                   pl.BlockSpec((tm, tn), lambda i, j: (i, jnp.clip(j - first_gate, 0, N_GATE_TILES - 1)))),
        scratch_shapes=[pltpu.VMEM((tm, d), BF16), pltpu.VMEM((tn // LANES, tm, LANES), F32)],
        compiler_params=_params(("arbitrary", "arbitrary"), 56),
        name="norm_proj",
    )(x, g, mod, mod, w_cat, col_scale)


def _proj_weights(w_in, w_branch_gate):
    cols = []
    for gi in range(len(A_GROUPS)):
        for part in range(3):
            start = part * (A_COLS // 3) + gi * A_GROUP_COLS
            cols.append(np.arange(start, start + A_GROUP_COLS))
    c0 = A_COLS + B_COLS
    cols.append(np.arange(c0, c0 + C_COLS))
    cols.append(np.arange(A_COLS, A_COLS + B_COLS))
    perm = np.concatenate(cols)
    w_cat = jnp.concatenate([w_in[..., perm], w_branch_gate], axis=-1).astype(BF16)

    scale = np.ones((2 * IN_COLS,), np.float32)
    for gi in range(len(A_GROUPS)):
        scale[gi * A_TILE:gi * A_TILE + A_GROUP_COLS] = A_HEAD_DIM ** -0.5 * LOG2E
    scale[A_COLS:A_COLS + C_OUT] = C_HEAD_DIM ** -0.5 * LOG2E
    qb = A_COLS + C_COLS
    scale[qb:qb + B_OUT] = B_QK_DIM ** -0.5 * LOG2E
    return w_cat, jnp.asarray(scale).reshape(1, -1)


MIX_C_Q, MIX_C_K, MIX_C_V = 0, C_OUT, C_OUT + C_KV_HEADS * C_HEAD_DIM
MIX_B_Q = C_COLS
MIX_B_K = MIX_B_Q + B_OUT
MIX_B_V = MIX_B_K + B_OUT


def _position_features(seq_len):
    j = np.arange(seq_len)
    feat = np.zeros((seq_len, LANES), np.float32)
    feat[:, 0:N_PIECES] = ((j // POS_SPLIT) * POS_SPLIT)[:, None]
    feat[:, N_PIECES:2 * N_PIECES] = (j % POS_SPLIT)[:, None]
    feat[:, 2 * N_PIECES:3 * N_PIECES] = 1.0
    return jnp.asarray(feat, BF16)


ONES_COL = B_V_DIM + 2 * N_PIECES


def _battn_kernel(slope_ref, q_ref, k_ref, v_ref, feat_ref, lq1_ref, lk1_ref, lq2_ref, lk2_ref,
                  gs_ref, o_ref, *, tq, tk, seq_len, lam_init):
    h = pl.program_id(1)
    qi = pl.program_id(2)
    slope = slope_ref[h] * LOG2E
    q = q_ref[...]
    lane = lax.broadcasted_iota(jnp.int32, (tq, LANES), 1)
    zero = jnp.zeros_like(q)
    qq = jnp.concatenate([jnp.where(lane < B_QK_DIM, q, zero),
                          jnp.where(lane >= B_QK_DIM, q, zero)], axis=0)
    row = lax.broadcasted_iota(jnp.int32, (2 * tq, 1), 0)
    qpos = qi * tq + jnp.where(row >= tq, row - tq, row)

    lane2 = lax.broadcasted_iota(jnp.int32, (2 * tq, LANES), 1)
    piece = lane2 % N_PIECES
    s_hi, s_mid, s_lo = _split3(jnp.full((1, LANES), slope, F32))
    c_hi, c_mid, c_lo = _split3(slope * qpos.astype(F32))
    s_piece = jnp.where(piece == 0, s_hi, jnp.where(piece == 1, s_mid, s_lo))
    c_piece = jnp.where(piece == 0, c_hi, jnp.where(piece == 1, c_mid, c_lo))
    aug = jnp.where(lane2 < 2 * N_PIECES, -s_piece, jnp.where(lane2 < 3 * N_PIECES, c_piece, 0.0))
    lhs_right = jnp.concatenate([qq, aug.astype(BF16)], axis=1)
    lhs_left = jnp.concatenate([qq, (-aug).astype(BF16)], axis=1)

    nchunks = seq_len // tk
    jd = (qi * tq) // tk

    def chunk(t, carry, diag):
        m, acc = carry
        j = lax.rem(jd + t, nchunks)
        k0 = pl.multiple_of(j * tk, tk)
        kc = k_ref[pl.ds(k0, tk), :]
        fc = feat_ref[pl.ds(k0, tk), :]
        vf = jnp.concatenate([v_ref[pl.ds(k0, tk), :], fc], axis=1)
        if diag:
            s = lax.dot_general(qq, kc, _NT_DIMS, preferred_element_type=F32)
            kpos = k0 + lax.broadcasted_iota(jnp.int32, (1, tk), 1)
            s = s - slope * jnp.abs(qpos - kpos).astype(F32)
        else:
            lhs = jnp.where(j < jd, lhs_left, lhs_right)
            s = lax.dot_general(lhs, jnp.concatenate([kc, fc], axis=1), _NT_DIMS,
                                preferred_element_type=F32)
        m_new = jnp.maximum(m, jnp.max(s, axis=-1, keepdims=True))
        alpha = jnp.exp2(m - m_new)
        p = jnp.exp2(s - m_new).astype(BF16)
        acc = alpha * acc + jnp.dot(p, vf, preferred_element_type=F32)
        return m_new, acc

    def pair(u, carry):
        carry = chunk(2 * u, carry, False)
        return chunk(2 * u + 1, carry, False)

    carry = (jnp.full((2 * tq, 1), NEG_INF, F32), jnp.zeros((2 * tq, 2 * LANES), F32))
    carry = chunk(0, carry, True)
    for t in range(1, nchunks):
        carry = chunk(t, carry, False)
    _, acc = carry

    o = acc[:, :B_V_DIM] / acc[:, ONES_COL:ONES_COL + 1]
    lam = (jnp.exp(jnp.sum(lq1_ref[...] * lk1_ref[...], axis=-1, keepdims=True))
           - jnp.exp(jnp.sum(lq2_ref[...] * lk2_ref[...], axis=-1, keepdims=True)) + lam_init)
    diff = o[:tq] - lam * o[tq:]
    ms = jnp.mean(diff * diff, axis=-1, keepdims=True)
    y = diff * lax.rsqrt(ms + RMS_EPS) * gs_ref[...]
    o_ref[...] = (y * (1.0 - lam_init)).astype(BF16)


def _diff_attention(mix, feat, slopes_b, lq1, lk1, lq2, lk2, g_subln, layer, batch, seq_len,
                    tq=256, tk=1024):
    n = mix.shape[0]
    tq = min(tq, seq_len)
    tk = min(tk, seq_len)
    nq = seq_len // tq
    qcol, kcol, vcol = (c // B_V_DIM for c in (MIX_B_Q, MIX_B_K, MIX_B_V))
    lam_init = 0.8 - 0.6 * math.exp(-0.3 * layer)
    kern = functools.partial(_battn_kernel, tq=tq, tk=tk, seq_len=seq_len, lam_init=lam_init)
    vec = lambda width: pl.BlockSpec((None, 1, width), lambda b, h, i: (layer, 0, 0))
    return pl.pallas_call(
        kern,
        out_shape=jax.ShapeDtypeStruct((n, B_OUT), BF16),
        grid=(batch, B_HEADS, nq),
        in_specs=[
            pl.BlockSpec(memory_space=pltpu.SMEM),
            pl.BlockSpec((tq, B_V_DIM), lambda b, h, i: (b * nq + i, qcol + h)),
            pl.BlockSpec((seq_len, B_V_DIM), lambda b, h, i: (b, kcol + h)),
            pl.BlockSpec((seq_len, B_V_DIM), lambda b, h, i: (b, vcol + h)),
            pl.BlockSpec((seq_len, LANES), lambda b, h, i: (0, 0)),
            vec(B_QK_DIM), vec(B_QK_DIM), vec(B_QK_DIM), vec(B_QK_DIM), vec(B_V_DIM),
        ],
        out_specs=pl.BlockSpec((tq, B_V_DIM), lambda b, h, i: (b * nq + i, h)),
        compiler_params=_params(("arbitrary", "arbitrary", "arbitrary"), 48),
        name="diff_attention",
    )(slopes_b, mix, mix, mix, feat, lq1, lk1, lq2, lk2, g_subln)


def _band_window(qi, tq, half, length):
    kw = tq + 2 * half
    ks = jnp.clip(qi * tq - half, 0, length - kw)
    ks = pl.multiple_of(ks, half)
    qpos = qi * tq + lax.broadcasted_iota(jnp.int32, (tq, 1), 0)
    kpos = ks + lax.broadcasted_iota(jnp.int32, (1, kw), 1)
    rel = jnp.abs(qpos - kpos)
    return ks, kw, rel <= half, rel.astype(F32)


def _cattn_kernel(slope_ref, sink_ref, q_ref, k_ref, v_ref, o_ref, *, tq, seq_len):
    g = pl.program_id(1)
    qi = pl.program_id(2)
    ks, kw, valid, relf = _band_window(qi, tq, C_HALF_WINDOW, seq_len)
    kwin = k_ref[pl.ds(ks, kw), :]
    vwin = v_ref[pl.ds(ks, kw), :]
    for hh in range(C_GROUP):
        head = g * C_GROUP + hh
        slope = slope_ref[head] * LOG2E
        sink = sink_ref[head] * LOG2E
        cols = slice(hh * C_HEAD_DIM, (hh + 1) * C_HEAD_DIM)
        s = lax.dot_general(q_ref[:, cols], kwin, _NT_DIMS, preferred_element_type=F32)
        s = jnp.where(valid, s - slope * relf, NEG_INF)
        m = jnp.maximum(jnp.max(s, axis=-1, keepdims=True), sink)
        p = jnp.exp2(s - m)
        den = jnp.sum(p, axis=-1, keepdims=True) + jnp.exp2(sink - m)
        o = jnp.dot(p.astype(BF16), vwin, preferred_element_type=F32) / den
        o_ref[:, cols] = o.astype(BF16)


def _window_attention(mix, slopes_c, sink, batch, seq_len, tq=256):
    n = mix.shape[0]
    nq = seq_len // tq
    gw = C_GROUP * C_HEAD_DIM
    qcol = MIX_C_Q // gw
    kcol = MIX_C_K // C_HEAD_DIM
    vcol = MIX_C_V // C_HEAD_DIM
    kern = functools.partial(_cattn_kernel, tq=tq, seq_len=seq_len)
    return pl.pallas_call(
        kern,
        out_shape=jax.ShapeDtypeStruct((n, C_OUT), BF16),
        grid=(batch, C_KV_HEADS, nq),
        in_specs=[
            pl.BlockSpec(memory_space=pltpu.SMEM),
            pl.BlockSpec(memory_space=pltpu.SMEM),
            pl.BlockSpec((tq, gw), lambda b, g, i: (b * nq + i, qcol + g)),
            pl.BlockSpec((seq_len, C_HEAD_DIM), lambda b, g, i: (b, kcol + g)),
            pl.BlockSpec((seq_len, C_HEAD_DIM), lambda b, g, i: (b, vcol + g)),
        ],
        out_specs=pl.BlockSpec((tq, gw), lambda b, g, i: (b * nq + i, g)),
        compiler_params=_params(("arbitrary", "arbitrary", "arbitrary"), 32),
        name="window_attention",
    )(slopes_c, sink, mix, mix, mix)


def _aattn_kernel(q_ref, k_ref, v_ref, o_ref, lse_ref, *scratch, tq, length, slopes, dil):
    qi = pl.program_id(1)
    nslab = A_GROUP_COLS // LANES
    half = A_GROUPS[0][0] // 2
    ks, kw, valid, relf = _band_window(qi, tq, half, length)
    lane = lax.broadcasted_iota(jnp.int32, (tq, A_GROUP_COLS), 1)

    def residue(r, _):
        kwin = k_ref[r, pl.ds(ks, kw), :]
        vwin = v_ref[r, pl.ds(ks, kw), :]
        q = q_ref[r]
        zero = jnp.zeros_like(q)
        out = jnp.zeros((tq, A_GROUP_COLS), F32)
        lse_out = jnp.zeros((tq, A_GROUP_COLS), F32)
        for hh in range(A_HEADS_PER_GROUP):
            in_head = (lane >= hh * A_HEAD_DIM) & (lane < (hh + 1) * A_HEAD_DIM)
            s = lax.dot_general(jnp.where(in_head, q, zero), kwin, _NT_DIMS,
                                preferred_element_type=F32)
            s = jnp.where(valid, s - (slopes[hh] * dil * LOG2E) * relf, NEG_INF)
            m = jnp.max(s, axis=-1, keepdims=True)
            p = jnp.exp2(s - m)
            den = jnp.sum(p, axis=-1, keepdims=True)
            o = jnp.dot(p.astype(BF16), vwin, preferred_element_type=F32) / den
            out = jnp.where(in_head, o, out)
            lse_out = jnp.where(in_head, m + jnp.log2(den), lse_out)
        if dil == 1:
            o_ref[...] = out
            lse_ref[...] = lse_out
        else:
            o_scr, lse_scr = scratch
            for c in range(nslab):
                cols = slice(c * LANES, (c + 1) * LANES)
                o_scr[c, pl.ds(r, tq, stride=dil), :] = out[:, cols]
                lse_scr[c, pl.ds(r, tq, stride=dil), :] = lse_out[:, cols]
        return 0

    if dil == 1:
        residue(0, 0)
    else:
        lax.fori_loop(0, dil, residue, 0)
        o_scr, lse_scr = scratch
        o_ref[...] = jnp.concatenate([o_scr[c] for c in range(nslab)], axis=1)
        lse_ref[...] = jnp.concatenate([lse_scr[c] for c in range(nslab)], axis=1)


def _dilated_attention(qkv, group, slopes, batch, seq_len, tq=256):
    window, dil = A_GROUPS[group]
    half = window // (2 * dil)
    length = seq_len // dil
    tq = min(tq, length - 2 * half)
    nq = length // tq
    n = batch * seq_len
    kern = functools.partial(_aattn_kernel, tq=tq, length=length,
                             slopes=tuple(float(v) for v in slopes), dil=dil)
    out_sds = jax.ShapeDtypeStruct((n, A_GROUP_COLS), F32)
    out_spec = pl.BlockSpec((tq * dil, A_GROUP_COLS), lambda b, i: (b * nq + i, 0))
    kv_spec = lambda part: pl.BlockSpec((None, dil, length, A_GROUP_COLS), lambda b, i: (b, 0, 0, part),
                                        pipeline_mode=pl.Buffered(1))
    return pl.pallas_call(
        kern,
        out_shape=(out_sds, out_sds),
        grid=(batch, nq),
        in_specs=[
            pl.BlockSpec((None, dil, tq, A_GROUP_COLS), lambda b, i: (b, 0, i, 0)),
            kv_spec(1), kv_spec(2),
        ],
        out_specs=(out_spec, out_spec),
        scratch_shapes=[] if dil == 1 else
        [pltpu.VMEM((A_GROUP_COLS // LANES, tq * dil, LANES), F32)] * 2,
        compiler_params=_params(("arbitrary", "arbitrary"), 48),
        name=f"dilated_attention_g{group}",
    )(qkv, qkv, qkv)


def _merge_kernel(x_ref, gt_ref, o0_ref, o1_ref, o2_ref, l0_ref, l1_ref, l2_ref, ob_ref, oc_ref,
                  ga_ref, gb_ref, gc_ref, wa_ref, wb_ref, wc_ref, wo_ref, out_ref):
    l0, l1, l2 = l0_ref[...], l1_ref[...], l2_ref[...]
    m = jnp.maximum(jnp.maximum(l0, l1), l2)
    e0, e1, e2 = jnp.exp2(l0 - m), jnp.exp2(l1 - m), jnp.exp2(l2 - m)
    oa = (e0 * o0_ref[...] + e1 * o1_ref[...] + e2 * o2_ref[...]) / (e0 + e1 + e2)
    ya = jnp.dot(oa.astype(BF16), wa_ref[...], preferred_element_type=F32)
    yb = jnp.dot(ob_ref[...], wb_ref[...], preferred_element_type=F32)
    yc = jnp.dot(oc_ref[...], wc_ref[...], preferred_element_type=F32)
    merged = (ga_ref[...].astype(F32) * ya + gb_ref[...].astype(F32) * yb
              + gc_ref[...].astype(F32) * yc)
    y = jnp.dot(merged.astype(BF16), wo_ref[...], preferred_element_type=F32)
    out_ref[...] = x_ref[...] + gt_ref[...] * y


def _merge(x, mod, oa_parts, ob, oc, gates, w_br_a, w_br_b, w_br_c, w_o, layer, seq_len, tm=256):
    n, d = x.shape
    tm = min(tm, seq_len)
    tiles_per_seq = seq_len // tm
    seq_of = lambda i: i // tiles_per_seq
    (o0, l0), (o1, l1), (o2, l2) = oa_parts
    row = lambda width: pl.BlockSpec((tm, width), lambda i: (i, 0))
    resident = lambda rows: pl.BlockSpec((None, rows, d), lambda i: (layer, 0, 0),
                                         pipeline_mode=pl.Buffered(1))
    gate = lambda k: pl.BlockSpec((tm, d), lambda i: (i, k))
    return pl.pallas_call(
        _merge_kernel,
        out_shape=jax.ShapeDtypeStruct((n, d), F32),
        grid=(n // tm,),
        in_specs=[
            row(d),
            pl.BlockSpec((None, None, None, 1, d), _mod_spec(layer, 2, seq_of)),
            row(A_OUT), row(A_OUT), row(A_OUT), row(A_OUT), row(A_OUT), row(A_OUT),
            row(B_OUT), row(C_OUT),
            gate(0), gate(1), gate(2),
            resident(A_OUT), resident(B_OUT), resident(C_OUT), resident(d),
        ],
        out_specs=row(d),
        compiler_params=_params(("arbitrary",), 52),
        name="merge",
    )(x, mod, o0, o1, o2, l0, l1, l2, ob, oc, gates, gates, gates, w_br_a, w_br_b, w_br_c, w_o)


def _router_kernel(x_ref, g_ref, sc_ref, sh_ref, wr_ref, h_ref, aff_ref):
    h = _rms_mod(x_ref[...], g_ref[...], sc_ref[...], sh_ref[...])
    h_ref[...] = h.astype(BF16)
    logits = lax.dot_general(wr_ref[...], h, _NT_DIMS, precision=lax.Precision.HIGHEST,
                             preferred_element_type=F32)
    m = jnp.max(logits, axis=0, keepdims=True)
    e = jnp.exp(logits - m)
    aff_ref[...] = e / jnp.sum(e, axis=0, keepdims=True)


def _norm_router(x, g, mod, w_router_t, layer, seq_len, tm=512):
    n, d = x.shape
    tm = min(tm, seq_len)
    tiles_per_seq = seq_len // tm
    seq_of = lambda i: i // tiles_per_seq
    ne = w_router_t.shape[1]
    return pl.pallas_call(
        _router_kernel,
        out_shape=(jax.ShapeDtypeStruct((n, d), BF16), jax.ShapeDtypeStruct((ne, n), F32)),
        grid=(n // tm,),
        in_specs=[
            pl.BlockSpec((tm, d), lambda i: (i, 0)),
            pl.BlockSpec((None, 1, d), lambda i: (layer, 0, 0)),
            pl.BlockSpec((None, None, None, 1, d), _mod_spec(layer, 4, seq_of)),
            pl.BlockSpec((None, None, None, 1, d), _mod_spec(layer, 3, seq_of)),
            pl.BlockSpec((None, ne, d), lambda i: (layer, 0, 0)),
        ],
        out_specs=(pl.BlockSpec((tm, d), lambda i: (i, 0)),
                   pl.BlockSpec((ne, tm), lambda i: (0, i))),
        compiler_params=_params(("arbitrary",), 32),
        name="norm_router",
    )(x, g, mod, mod, w_router_t)


def _ffn_kernel(x_ref, gate_ref, wg_ref, wu_ref, wd_ref, o_ref):
    f = pl.program_id(2)
    x = x_ref[...]
    a = jnp.dot(x, wg_ref[...], preferred_element_type=F32)
    u = jnp.dot(x, wu_ref[...], preferred_element_type=F32)
    hid = (a * jax.nn.sigmoid(a) * u).astype(BF16)
    part = jnp.dot(hid, wd_ref[...], preferred_element_type=F32)

    @pl.when(f == 0)
    def _():
        o_ref[...] = part

    @pl.when(f > 0)
    def _():
        o_ref[...] += part

    @pl.when(f == pl.num_programs(2) - 1)
    def _():
        o_ref[...] = o_ref[...] * gate_ref[...]


def _expert_ffn(xe, gate, w_gate, w_up, w_down, layer, tm=1024, tf=512):
    ne, cap, d = xe.shape
    dff = w_gate.shape[-1]
    tm = min(tm, cap)
    return pl.pallas_call(
        _ffn_kernel,
        out_shape=jax.ShapeDtypeStruct((ne, cap, d), F32),
        grid=(ne, cap // tm, dff // tf),
        in_specs=[
            pl.BlockSpec((None, tm, d), lambda e, i, f: (e, i, 0)),
            pl.BlockSpec((None, tm, 1), lambda e, i, f: (e, i, 0)),
            pl.BlockSpec((None, None, d, tf), lambda e, i, f: (layer, e, 0, f)),
            pl.BlockSpec((None, None, d, tf), lambda e, i, f: (layer, e, 0, f)),
            pl.BlockSpec((None, None, tf, d), lambda e, i, f: (layer, e, f, 0)),
        ],
        out_specs=pl.BlockSpec((None, tm, d), lambda e, i, f: (e, i, 0)),
        compiler_params=_params(("arbitrary", "arbitrary", "arbitrary"), 48),
        name="expert_ffn",
    )(xe, gate, w_gate, w_up, w_down)


def _residual_kernel(x_ref, y_ref, gt_ref, gf_ref, o_ref, *, final_norm):
    x = x_ref[...] + gt_ref[...] * y_ref[...]
    if final_norm:
        ms = jnp.mean(x * x, axis=-1, keepdims=True)
        x = x * lax.rsqrt(ms + RMS_EPS) * gf_ref[...]
    o_ref[...] = x


def _residual(x, y, mod, g_final, layer, seq_len, final_norm, tm=512):
    n, d = x.shape
    tm = min(tm, seq_len)
    tiles_per_seq = seq_len // tm
    seq_of = lambda i: i // tiles_per_seq
    kern = functools.partial(_residual_kernel, final_norm=final_norm)
    row = pl.BlockSpec((tm, d), lambda i: (i, 0))
    return pl.pallas_call(
        kern,
        out_shape=jax.ShapeDtypeStruct((n, d), F32),
        grid=(n // tm,),
        in_specs=[row, row,
                  pl.BlockSpec((None, None, None, 1, d), _mod_spec(layer, 5, seq_of)),
                  pl.BlockSpec((1, d), lambda i: (0, 0))],
        out_specs=row,
        compiler_params=_params(("arbitrary",), 40),
        name="residual",
    )(x, y, mod, g_final)


def _trunk(x, c_rows, group_sizes, seq_len, g_norm1, g_norm2, w_ada, b_ada, w_in, w_branch_gate,
           w_br_a, w_br_b, w_br_c, w_o, lambda_q1, lambda_k1, lambda_q2, lambda_k2, g_subln, sink,
           w_router, w_e_gate, w_e_up, w_e_down, g_final):
    n, d = x.shape
    depth = w_in.shape[0]
    batch = n // seq_len
    s_a, s_b, s_c = _alibi_slopes()

    w_cat, col_scale = _proj_weights(w_in, w_branch_gate)
    wa, wb, wc, wo = (w.astype(BF16) for w in (w_br_a, w_br_b, w_br_c, w_o))
    weg, weu, wed = (w.astype(BF16) for w in (w_e_gate, w_e_up, w_e_down))
    w_router_t = jnp.swapaxes(w_router, 1, 2)
    row3 = lambda a: a.reshape(depth, 1, a.shape[-1])
    g1, g2 = row3(g_norm1), row3(g_norm2)
    lq1, lk1, lq2, lk2, gsub = (row3(a) for a in (lambda_q1, lambda_k1, lambda_q2, lambda_k2, g_subln))
    feat = _position_features(seq_len)

    mod = _ada_modulation(c_rows, w_ada, b_ada)
    mod = mod.reshape(depth, c_rows.shape[0], N_MOD, 1, d)

    for layer in range(depth):
        *a_qkv, mix, gates = _norm_proj(x, g1, mod, w_cat, col_scale, layer, batch, seq_len)
        oa_parts = [
            _dilated_attention(a_qkv[gi], gi,
                               s_a[gi * A_HEADS_PER_GROUP:(gi + 1) * A_HEADS_PER_GROUP],
                               batch, seq_len)
            for gi in range(len(A_GROUPS))
        ]
        ob = _diff_attention(mix, feat, jnp.asarray(s_b), lq1, lk1, lq2, lk2, gsub, layer, batch,
                             seq_len)
        oc = _window_attention(mix, jnp.asarray(s_c), sink[layer], batch, seq_len)
        x = _merge(x, mod, oa_parts, ob, oc, gates, wa, wb, wc, wo, layer, seq_len)

        h2, aff_t = _norm_router(x, g2, mod, w_router_t, layer, seq_len)
        idx_parts, gate_parts = [], []
        start = 0
        for size in group_sizes:
            cap = (EC_CAPACITY_FACTOR * size) // N_EXPERTS
            gate, idx = lax.top_k(aff_t[:, start:start + size], cap)
            idx_parts.append(idx + start)
            gate_parts.append(gate)
            start += size
        idx = jnp.concatenate(idx_parts, axis=1)
        gate = jnp.concatenate(gate_parts, axis=1)
        xe = jnp.take(h2, idx, axis=0)
        ye = _expert_ffn(xe, gate[..., None], weg, weu, wed, layer)
        y = jnp.zeros((n, d), F32).at[idx.reshape(-1)].add(ye.reshape(-1, d))
        x = _residual(x, y, mod, g_final.reshape(1, d), layer, seq_len, layer == depth - 1)
    return x


def kernel(x_prompt, x_sample, c_prompt, c_sample, g_norm1, g_norm2, w_ada, b_ada, w_in, w_branch_gate, w_br_a, w_br_b, w_br_c, w_o, lambda_q1, lambda_k1, lambda_q2, lambda_k2, g_subln, sink, w_router, w_e_gate, w_e_up, w_e_down, g_final):
    bp, seq_len, d = x_prompt.shape
    bs = x_sample.shape[0]
    assert x_sample.shape[1] == seq_len
    x = jnp.concatenate([x_prompt.reshape(bp * seq_len, d), x_sample.reshape(bs * seq_len, d)], axis=0)
    c = jnp.concatenate([c_prompt, c_sample], axis=0)
    pad = -c.shape[0] % SUBLANES
    c_rows = jnp.pad(c, ((0, pad), (0, 0)))
    y = _trunk(x, c_rows, (bp * seq_len, bs * seq_len), seq_len, g_norm1, g_norm2, w_ada, b_ada,
               w_in, w_branch_gate, w_br_a, w_br_b, w_br_c, w_o, lambda_q1, lambda_k1, lambda_q2,
               lambda_k2, g_subln, sink, w_router, w_e_gate, w_e_up, w_e_down, g_final)
    y_prompt = y[:bp * seq_len].reshape(bp, seq_len, d)
    y_sample = y[bp * seq_len:].reshape(bs, seq_len, d)
    return (y_prompt, y_sample)
```

```python
import functools
import math

import numpy as np
import jax
import jax.numpy as jnp
from jax import lax
from jax.experimental import pallas as pl
from jax.experimental.pallas import tpu as pltpu

F32 = jnp.float32
BF16 = jnp.bfloat16

A_GROUPS = ((128, 1), (512, 4), (2048, 16))
A_HEADS_PER_GROUP = 4
A_HEAD_DIM = 64
A_N_HEADS = 12
A_OUT = 256
A_GROUP_COLS = A_HEADS_PER_GROUP * A_HEAD_DIM
A_TILE = 3 * A_GROUP_COLS
B_HEADS = 6
B_QK_DIM = 64
B_V_DIM = 128
B_OUT = 768
C_Q_HEADS = 8
C_KV_HEADS = 2
C_GROUP = C_Q_HEADS // C_KV_HEADS
C_HEAD_DIM = 128
C_HALF_WINDOW = 128
C_OUT = 1024
A_COLS = 2304
B_COLS = 2304
C_COLS = 1536
IN_COLS = 6144
N_BRANCH = 3
N_EXPERTS = 16
EC_CAPACITY_FACTOR = 2
N_ALIBI_HEADS = 26
RMS_EPS = 1e-6
NEG_INF = -1e30
LOG2E = math.log2(math.e)
N_MOD = 6
SUBLANES = 8
LANES = 128
POS_SPLIT = 64
N_PIECES = 3

_NT_DIMS = (((1,), (1,)), ((), ()))
_MIB = 1024 * 1024


def _alibi_slopes():
    n = N_ALIBI_HEADS
    s = 2.0 ** (-8.0 * np.arange(1, n + 1, dtype=np.float32) / n)
    s = s.astype(np.float32)
    s_c = s[:C_Q_HEADS]
    s_a = s[C_Q_HEADS:C_Q_HEADS + A_N_HEADS]
    s_b = s[C_Q_HEADS + A_N_HEADS:]
    return s_a, s_b, s_c


def _params(semantics, vmem_mib):
    return pltpu.CompilerParams(dimension_semantics=semantics,
                                vmem_limit_bytes=vmem_mib * _MIB)


def _mod_spec(layer, chunk, seq_of):
    def index(*ids):
        return (layer, seq_of(*ids), chunk, 0, 0)
    return index


def _rms_mod(x, g, scale, shift):
    ms = jnp.mean(x * x, axis=-1, keepdims=True)
    y = x * lax.rsqrt(ms + RMS_EPS) * g
    return y * (1.0 + scale) + shift


def _split3(x):
    hi = x.astype(BF16).astype(F32)
    mid = (x - hi).astype(BF16).astype(F32)
    lo = (x - hi - mid).astype(BF16).astype(F32)
    return hi, mid, lo


def _ada_kernel(c_ref, w_ref, b_ref, o_ref):
    c = c_ref[...]
    a = c * jax.nn.sigmoid(c)
    o_ref[...] = jnp.dot(a, w_ref[...], precision=lax.Precision.HIGHEST,
                         preferred_element_type=F32) + b_ref[...]


def _ada_modulation(c_rows, w_ada, b_ada, tn=1024):
    depth, d, n6 = w_ada.shape
    rows = c_rows.shape[0]
    return pl.pallas_call(
        _ada_kernel,
        out_shape=jax.ShapeDtypeStruct((depth, rows, n6), F32),
        grid=(depth, n6 // tn),
        in_specs=[
            pl.BlockSpec((rows, d), lambda l, j: (0, 0)),
            pl.BlockSpec((None, d, tn), lambda l, j: (l, 0, j)),
            pl.BlockSpec((None, 1, tn), lambda l, j: (l, 0, j)),
        ],
        out_specs=pl.BlockSpec((None, rows, tn), lambda l, j: (l, 0, j)),
        compiler_params=_params(("arbitrary", "arbitrary"), 40),
        name="ada_modulation",
    )(c_rows, w_ada, b_ada.reshape(depth, 1, n6))


def _norm_kernel(x_ref, g_ref, sc_ref, sh_ref, h_ref):
    h_ref[...] = _rms_mod(x_ref[...], g_ref[...], sc_ref[...], sh_ref[...]).astype(BF16)


def _norm_mod(x, g, mod, layer, seq_len, tm=512):
    n, d = x.shape
    tm = min(tm, seq_len)
    tps = seq_len // tm
    seq_of = lambda i: i // tps
    row = pl.BlockSpec((tm, d), lambda i: (i, 0))
    return pl.pallas_call(
        _norm_kernel,
        out_shape=jax.ShapeDtypeStruct((n, d), BF16),
        grid=(n // tm,),
        in_specs=[row,
                  pl.BlockSpec((None, 1, d), lambda i: (layer, 0, 0)),
                  pl.BlockSpec((None, None, None, 1, d), _mod_spec(layer, 1, seq_of)),
                  pl.BlockSpec((None, None, None, 1, d), _mod_spec(layer, 0, seq_of))],
        out_specs=row,
        compiler_params=_params(("arbitrary",), 32),
        name="norm_mod",
    )(x, g, mod, mod)


PROJ_TILE = A_TILE
N_A_TILES = len(A_GROUPS)
N_GATE_TILES = IN_COLS // PROJ_TILE
N_MIX_TILES = (C_COLS + B_COLS) // PROJ_TILE
MG_C_Q = IN_COLS
MG_C_K = MG_C_Q + C_OUT
MG_C_V = MG_C_K + C_KV_HEADS * C_HEAD_DIM
MG_B_Q = MG_C_Q + C_COLS
MG_B_K = MG_B_Q + B_OUT
MG_B_V = MG_B_K + B_OUT


def _proj_kernel(h_ref, w_ref, cs_ref, a0_ref, a1_ref, a2_ref, mg_ref, acc_scr, *, tm):
    j = pl.program_id(1)
    acc = jnp.dot(h_ref[...], w_ref[...], preferred_element_type=F32) * cs_ref[...]

    @pl.when(j == 0)
    def _():
        a0_ref[0] = acc.astype(BF16)

    for gi, a_ref in ((1, a1_ref), (2, a2_ref)):
        dil = A_GROUPS[gi][1]

        @pl.when(j == gi)
        def _(a_ref=a_ref, dil=dil):
            nslab = PROJ_TILE // LANES
            for c in range(nslab):
                acc_scr[c] = acc[:, c * LANES:(c + 1) * LANES]
            for r in range(dil):
                rows = [acc_scr[c, pl.ds(r, tm // dil, stride=dil), :] for c in range(nslab)]
                a_ref[r] = jnp.concatenate(rows, axis=1).astype(BF16)

    is_gate = (j >= N_A_TILES) & (j < N_A_TILES + N_GATE_TILES)
    mg_ref[...] = jnp.where(is_gate, 0.5 * jnp.tanh(0.5 * acc) + 0.5, acc).astype(BF16)


def _proj(h, w_cat, col_scale, layer, batch, seq_len, tm=1024):
    n, d = h.shape
    tn = PROJ_TILE
    tm = min(tm, seq_len)
    tps = seq_len // tm
    a_shapes, a_specs = [], []
    for _, dil in A_GROUPS:
        a_shapes.append(jax.ShapeDtypeStruct((batch, dil, seq_len // dil, tn), BF16))
        a_specs.append(pl.BlockSpec((None, dil, tm // dil, tn), lambda i, j: (i // tps, 0, i % tps, 0)))
    n_mg = N_GATE_TILES + N_MIX_TILES
    kern = functools.partial(_proj_kernel, tm=tm)
    return pl.pallas_call(
        kern,
        out_shape=(*a_shapes, jax.ShapeDtypeStruct((n, n_mg * tn), BF16)),
        grid=(n // tm, N_A_TILES + n_mg),
        in_specs=[
            pl.BlockSpec((tm, d), lambda i, j: (i, 0)),
            pl.BlockSpec((None, d, tn), lambda i, j: (layer, 0, j)),
            pl.BlockSpec((1, tn), lambda i, j: (0, j)),
        ],
        out_specs=(*a_specs,
                   pl.BlockSpec((tm, tn), lambda i, j: (i, jnp.clip(j - N_A_TILES, 0, n_mg - 1)))),
        scratch_shapes=[pltpu.VMEM((tn // LANES, tm, LANES), F32)],
        compiler_params=_params(("arbitrary", "arbitrary"), 48),
        name="proj",
    )(h, w_cat, col_scale)


def _proj_weights(w_in, w_branch_gate):
    cols = []
    for gi in range(len(A_GROUPS)):
        for part in range(3):
            start = part * (A_COLS // 3) + gi * A_GROUP_COLS
            cols.append(np.arange(start, start + A_GROUP_COLS))
    perm_a = np.concatenate(cols)
    c0 = A_COLS + B_COLS
    w_cat = jnp.concatenate([w_in[..., perm_a], w_branch_gate, w_in[..., c0:c0 + C_COLS],
                             w_in[..., A_COLS:A_COLS + B_COLS]], axis=-1).astype(BF16)

    scale = np.ones((2 * IN_COLS,), np.float32)
    for gi in range(len(A_GROUPS)):
        scale[gi * A_TILE:gi * A_TILE + A_GROUP_COLS] = A_HEAD_DIM ** -0.5 * LOG2E
    qc = A_COLS + MG_C_Q
    scale[qc:qc + C_OUT] = C_HEAD_DIM ** -0.5 * LOG2E
    qb = A_COLS + MG_B_Q
    scale[qb:qb + B_OUT] = B_QK_DIM ** -0.5 * LOG2E
    return w_cat, jnp.asarray(scale).reshape(1, -1)


def _position_features(seq_len):
    j = np.arange(seq_len)
    feat = np.zeros((seq_len, LANES), np.float32)
    feat[:, 0:N_PIECES] = ((j // POS_SPLIT) * POS_SPLIT)[:, None]
    feat[:, N_PIECES:2 * N_PIECES] = (j % POS_SPLIT)[:, None]
    feat[:, 2 * N_PIECES:3 * N_PIECES] = 1.0
    return jnp.asarray(feat, BF16)


ONES_COL = B_V_DIM + 2 * N_PIECES


def _battn_kernel(slope_ref, q_ref, k_ref, v_ref, feat_ref, lq1_ref, lk1_ref, lq2_ref, lk2_ref,
                  gs_ref, o_ref, *, tq, tk, seq_len, lam_init):
    h = pl.program_id(1)
    qi = pl.program_id(2)
    slope = slope_ref[h] * LOG2E
    q = q_ref[...]
    lane = lax.broadcasted_iota(jnp.int32, (tq, LANES), 1)
    zero = jnp.zeros_like(q)
    qq = jnp.concatenate([jnp.where(lane < B_QK_DIM, q, zero),
                          jnp.where(lane >= B_QK_DIM, q, zero)], axis=0)
    row = lax.broadcasted_iota(jnp.int32, (2 * tq, 1), 0)
    qpos = qi * tq + jnp.where(row >= tq, row - tq, row)

    lane2 = lax.broadcasted_iota(jnp.int32, (2 * tq, LANES), 1)
    piece = lane2 % N_PIECES
    s_hi, s_mid, s_lo = _split3(jnp.full((1, LANES), slope, F32))
    c_hi, c_mid, c_lo = _split3(slope * qpos.astype(F32))
    s_piece = jnp.where(piece == 0, s_hi, jnp.where(piece == 1, s_mid, s_lo))
    c_piece = jnp.where(piece == 0, c_hi, jnp.where(piece == 1, c_mid, c_lo))
    aug = jnp.where(lane2 < 2 * N_PIECES, -s_piece, jnp.where(lane2 < 3 * N_PIECES, c_piece, 0.0))
    lhs_right = jnp.concatenate([qq, aug.astype(BF16)], axis=1)
    lhs_left = jnp.concatenate([qq, (-aug).astype(BF16)], axis=1)

    nchunks = seq_len // tk
    jd = (qi * tq) // tk

    def chunk(t, carry, diag):
        m, acc = carry
        j = lax.rem(jd + t, nchunks)
        k0 = pl.multiple_of(j * tk, tk)
        kc = k_ref[pl.ds(k0, tk), :]
        fc = feat_ref[pl.ds(k0, tk), :]
        vf = jnp.concatenate([v_ref[pl.ds(k0, tk), :], fc], axis=1)
        if diag:
            s = lax.dot_general(qq, kc, _NT_DIMS, preferred_element_type=F32)
            kpos = k0 + lax.broadcasted_iota(jnp.int32, (1, tk), 1)
            s = s - slope * jnp.abs(qpos - kpos).astype(F32)
        else:
            lhs = jnp.where(j < jd, lhs_left, lhs_right)
            s = lax.dot_general(lhs, jnp.concatenate([kc, fc], axis=1), _NT_DIMS,
                                preferred_element_type=F32)
        m_new = jnp.maximum(m, jnp.max(s, axis=-1, keepdims=True))
        alpha = jnp.exp2(m - m_new)
        p = jnp.exp2(s - m_new).astype(BF16)
        acc = alpha * acc + jnp.dot(p, vf, preferred_element_type=F32)
        return m_new, acc

    carry = (jnp.full((2 * tq, 1), NEG_INF, F32), jnp.zeros((2 * tq, 2 * LANES), F32))
    carry = chunk(0, carry, True)
    for t in range(1, nchunks):
        carry = chunk(t, carry, False)
    _, acc = carry

    o = acc[:, :B_V_DIM] / acc[:, ONES_COL:ONES_COL + 1]
    lam = (jnp.exp(jnp.sum(lq1_ref[...] * lk1_ref[...], axis=-1, keepdims=True))
           - jnp.exp(jnp.sum(lq2_ref[...] * lk2_ref[...], axis=-1, keepdims=True)) + lam_init)
    diff = o[:tq] - lam * o[tq:]
    ms = jnp.mean(diff * diff, axis=-1, keepdims=True)
    y = diff * lax.rsqrt(ms + RMS_EPS) * gs_ref[...]
    o_ref[...] = (y * (1.0 - lam_init)).astype(BF16)


def _diff_attention(mix, feat, slopes_b, lq1, lk1, lq2, lk2, g_subln, layer, batch, seq_len,
                    tq=256, tk=1024):
    n = mix.shape[0]
    tq = min(tq, seq_len)
    tk = min(tk, seq_len)
    nq = seq_len // tq
    qcol, kcol, vcol = (c // B_V_DIM for c in (MG_B_Q, MG_B_K, MG_B_V))
    lam_init = 0.8 - 0.6 * math.exp(-0.3 * layer)
    kern = functools.partial(_battn_kernel, tq=tq, tk=tk, seq_len=seq_len, lam_init=lam_init)
    vec = lambda width: pl.BlockSpec((None, 1, width), lambda b, h, i: (layer, 0, 0))
    return pl.pallas_call(
        kern,
        out_shape=jax.ShapeDtypeStruct((n, B_OUT), BF16),
        grid=(batch, B_HEADS, nq),
        in_specs=[
            pl.BlockSpec(memory_space=pltpu.SMEM),
            pl.BlockSpec((tq, B_V_DIM), lambda b, h, i: (b * nq + i, qcol + h)),
            pl.BlockSpec((seq_len, B_V_DIM), lambda b, h, i: (b, kcol + h)),
            pl.BlockSpec((seq_len, B_V_DIM), lambda b, h, i: (b, vcol + h)),
            pl.BlockSpec((seq_len, LANES), lambda b, h, i: (0, 0)),
            vec(B_QK_DIM), vec(B_QK_DIM), vec(B_QK_DIM), vec(B_QK_DIM), vec(B_V_DIM),
        ],
        out_specs=pl.BlockSpec((tq, B_V_DIM), lambda b, h, i: (b * nq + i, h)),
        compiler_params=_params(("arbitrary", "arbitrary", "arbitrary"), 48),
        name="diff_attention",
    )(slopes_b, mix, mix, mix, feat, lq1, lk1, lq2, lk2, g_subln)


def _band_window(qi, tq, half, length):
    kw = tq + 2 * half
    ks = jnp.clip(qi * tq - half, 0, length - kw)
    ks = pl.multiple_of(ks, half)
    qpos = qi * tq + lax.broadcasted_iota(jnp.int32, (tq, 1), 0)
    kpos = ks + lax.broadcasted_iota(jnp.int32, (1, kw), 1)
    rel = jnp.abs(qpos - kpos)
    return ks, kw, rel <= half, rel.astype(F32)


def _cattn_kernel(slope_ref, sink_ref, q_ref, k_ref, v_ref, o_ref, *, tq, seq_len):
    g = pl.program_id(1)
    qi = pl.program_id(2)
    ks, kw, valid, relf = _band_window(qi, tq, C_HALF_WINDOW, seq_len)
    kwin = k_ref[pl.ds(ks, kw), :]
    vwin = v_ref[pl.ds(ks, kw), :]
    for hh in range(C_GROUP):
        head = g * C_GROUP + hh
        slope = slope_ref[head] * LOG2E
        sink = sink_ref[head] * LOG2E
        cols = slice(hh * C_HEAD_DIM, (hh + 1) * C_HEAD_DIM)
        s = lax.dot_general(q_ref[:, cols], kwin, _NT_DIMS, preferred_element_type=F32)
        s = jnp.where(valid, s - slope * relf, NEG_INF)
        m = jnp.maximum(jnp.max(s, axis=-1, keepdims=True), sink)
        p = jnp.exp2(s - m)
        den = jnp.sum(p, axis=-1, keepdims=True) + jnp.exp2(sink - m)
        o = jnp.dot(p.astype(BF16), vwin, preferred_element_type=F32) / den
        o_ref[:, cols] = o.astype(BF16)


def _window_attention(mix, slopes_c, sink, batch, seq_len, tq=256):
    n = mix.shape[0]
    nq = seq_len // tq
    gw = C_GROUP * C_HEAD_DIM
    qcol = MG_C_Q // gw
    kcol = MG_C_K // C_HEAD_DIM
    vcol = MG_C_V // C_HEAD_DIM
    kern = functools.partial(_cattn_kernel, tq=tq, seq_len=seq_len)
    return pl.pallas_call(
        kern,
        out_shape=jax.ShapeDtypeStruct((n, C_OUT), BF16),
        grid=(batch, C_KV_HEADS, nq),
        in_specs=[
            pl.BlockSpec(memory_space=pltpu.SMEM),
            pl.BlockSpec(memory_space=pltpu.SMEM),
            pl.BlockSpec((tq, gw), lambda b, g, i: (b * nq + i, qcol + g)),
            pl.BlockSpec((seq_len, C_HEAD_DIM), lambda b, g, i: (b, kcol + g)),
            pl.BlockSpec((seq_len, C_HEAD_DIM), lambda b, g, i: (b, vcol + g)),
        ],
        out_specs=pl.BlockSpec((tq, gw), lambda b, g, i: (b * nq + i, g)),
        compiler_params=_params(("arbitrary", "arbitrary", "arbitrary"), 32),
        name="window_attention",
    )(slopes_c, sink, mix, mix, mix)


def _aattn_kernel(q_ref, k_ref, v_ref, o_ref, lse_ref, *scratch, tq, length, slopes, dil):
    qi = pl.program_id(1)
    nslab = A_GROUP_COLS // LANES
    half = A_GROUPS[0][0] // 2
    ks, kw, valid, relf = _band_window(qi, tq, half, length)
    lane = lax.broadcasted_iota(jnp.int32, (tq, A_GROUP_COLS), 1)

    def residue(r, _):
        kwin = k_ref[r, pl.ds(ks, kw), :]
        vwin = v_ref[r, pl.ds(ks, kw), :]
        q = q_ref[r]
        zero = jnp.zeros_like(q)
        out = jnp.zeros((tq, A_GROUP_COLS), F32)
        lse_out = jnp.zeros((tq, A_GROUP_COLS), F32)
        for hh in range(A_HEADS_PER_GROUP):
            in_head = (lane >= hh * A_HEAD_DIM) & (lane < (hh + 1) * A_HEAD_DIM)
            s = lax.dot_general(jnp.where(in_head, q, zero), kwin, _NT_DIMS,
                                preferred_element_type=F32)
            s = jnp.where(valid, s - (slopes[hh] * dil * LOG2E) * relf, NEG_INF)
            m = jnp.max(s, axis=-1, keepdims=True)
            p = jnp.exp2(s - m)
            den = jnp.sum(p, axis=-1, keepdims=True)
            o = jnp.dot(p.astype(BF16), vwin, preferred_element_type=F32) / den
            out = jnp.where(in_head, o, out)
            lse_out = jnp.where(in_head, m + jnp.log2(den), lse_out)
        if dil == 1:
            o_ref[...] = out
            lse_ref[...] = lse_out
        else:
            o_scr, lse_scr = scratch
            for c in range(nslab):
                cols = slice(c * LANES, (c + 1) * LANES)
                o_scr[c, pl.ds(r, tq, stride=dil), :] = out[:, cols]
                lse_scr[c, pl.ds(r, tq, stride=dil), :] = lse_out[:, cols]
        return 0

    if dil == 1:
        residue(0, 0)
    else:
        lax.fori_loop(0, dil, residue, 0)
        o_scr, lse_scr = scratch
        o_ref[...] = jnp.concatenate([o_scr[c] for c in range(nslab)], axis=1)
        lse_ref[...] = jnp.concatenate([lse_scr[c] for c in range(nslab)], axis=1)


def _dilated_attention(qkv, group, slopes, batch, seq_len, tq=256):
    window, dil = A_GROUPS[group]
    half = window // (2 * dil)
    length = seq_len // dil
    tq = min(tq, length - 2 * half)
    nq = length // tq
    n = batch * seq_len
    kern = functools.partial(_aattn_kernel, tq=tq, length=length,
                             slopes=tuple(float(v) for v in slopes), dil=dil)
    out_sds = jax.ShapeDtypeStruct((n, A_GROUP_COLS), F32)
    out_spec = pl.BlockSpec((tq * dil, A_GROUP_COLS), lambda b, i: (b * nq + i, 0))
    kv_spec = lambda part: pl.BlockSpec((None, dil, length, A_GROUP_COLS), lambda b, i: (b, 0, 0, part),
                                        pipeline_mode=pl.Buffered(1))
    return pl.pallas_call(
        kern,
        out_shape=(out_sds, out_sds),
        grid=(batch, nq),
        in_specs=[
            pl.BlockSpec((None, dil, tq, A_GROUP_COLS), lambda b, i: (b, 0, i, 0)),
            kv_spec(1), kv_spec(2),
        ],
        out_specs=(out_spec, out_spec),
        scratch_shapes=[] if dil == 1 else
        [pltpu.VMEM((A_GROUP_COLS // LANES, tq * dil, LANES), F32)] * 2,
        compiler_params=_params(("arbitrary", "arbitrary"), 48),
        name=f"dilated_attention_g{group}",
    )(qkv, qkv, qkv)


def _merge_kernel(x_ref, gt_ref, o0_ref, o1_ref, o2_ref, l0_ref, l1_ref, l2_ref, ob_ref, oc_ref,
                  ga_ref, gb_ref, gc_ref, wa_ref, wb_ref, wc_ref, wo_ref, g2_ref, sc2_ref, sh2_ref,
                  wr_ref, out_ref, h_ref, aff_ref):
    l0, l1, l2 = l0_ref[...], l1_ref[...], l2_ref[...]
    m = jnp.maximum(jnp.maximum(l0, l1), l2)
    e0, e1, e2 = jnp.exp2(l0 - m), jnp.exp2(l1 - m), jnp.exp2(l2 - m)
    oa = (e0 * o0_ref[...] + e1 * o1_ref[...] + e2 * o2_ref[...]) / (e0 + e1 + e2)
    ya = jnp.dot(oa.astype(BF16), wa_ref[...], preferred_element_type=F32)
    yb = jnp.dot(ob_ref[...], wb_ref[...], preferred_element_type=F32)
    yc = jnp.dot(oc_ref[...], wc_ref[...], preferred_element_type=F32)
    merged = (ga_ref[...].astype(F32) * ya + gb_ref[...].astype(F32) * yb
              + gc_ref[...].astype(F32) * yc)
    y = jnp.dot(merged.astype(BF16), wo_ref[...], preferred_element_type=F32)
    x = x_ref[...] + gt_ref[...] * y
    out_ref[...] = x

    h = _rms_mod(x, g2_ref[...], sc2_ref[...], sh2_ref[...])
    h_ref[...] = h.astype(BF16)
    logits = lax.dot_general(wr_ref[...], h, _NT_DIMS, precision=lax.Precision.HIGHEST,
                             preferred_element_type=F32)
    mx = jnp.max(logits, axis=0, keepdims=True)
    e = jnp.exp(logits - mx)
    aff_ref[...] = e / jnp.sum(e, axis=0, keepdims=True)


def _merge(x, mod, oa_parts, ob, oc, mg, w_br_a, w_br_b, w_br_c, w_o, g2, w_router_t, layer,
           seq_len, tm=256):
    n, d = x.shape
    tm = min(tm, seq_len)
    tiles_per_seq = seq_len // tm
    seq_of = lambda i: i // tiles_per_seq
    ne = w_router_t.shape[1]
    (o0, l0), (o1, l1), (o2, l2) = oa_parts
    row = lambda width: pl.BlockSpec((tm, width), lambda i: (i, 0))
    resident = lambda rows: pl.BlockSpec((None, rows, d), lambda i: (layer, 0, 0),
                                         pipeline_mode=pl.Buffered(1))
    gate = lambda k: pl.BlockSpec((tm, d), lambda i: (i, k))
    modrow = lambda chunk: pl.BlockSpec((None, None, None, 1, d), _mod_spec(layer, chunk, seq_of))
    return pl.pallas_call(
        _merge_kernel,
        out_shape=(jax.ShapeDtypeStruct((n, d), F32), jax.ShapeDtypeStruct((n, d), BF16),
                   jax.ShapeDtypeStruct((ne, n), F32)),
        grid=(n // tm,),
        in_specs=[
            row(d), modrow(2),
            row(A_OUT), row(A_OUT), row(A_OUT), row(A_OUT), row(A_OUT), row(A_OUT),
            row(B_OUT), row(C_OUT),
            gate(0), gate(1), gate(2),
            resident(A_OUT), resident(B_OUT), resident(C_OUT), resident(d),
            pl.BlockSpec((None, 1, d), lambda i: (layer, 0, 0)), modrow(4), modrow(3),
            pl.BlockSpec((None, ne, d), lambda i: (layer, 0, 0)),
        ],
        out_specs=(row(d), row(d), pl.BlockSpec((ne, tm), lambda i: (0, i))),
        compiler_params=_params(("arbitrary",), 56),
        name="merge",
    )(x, mod, o0, o1, o2, l0, l1, l2, ob, oc, mg, mg, mg, w_br_a, w_br_b, w_br_c, w_o,
      g2, mod, mod, w_router_t)


def _ffn_kernel(x_ref, gate_ref, wg_ref, wu_ref, wd_ref, o_ref, acc_scr):
    f = pl.program_id(2)
    x = x_ref[...]
    a = jnp.dot(x, wg_ref[...], preferred_element_type=F32)
    u = jnp.dot(x, wu_ref[...], preferred_element_type=F32)
    hid = (a * jax.nn.sigmoid(a) * u).astype(BF16)
    part = jnp.dot(hid, wd_ref[...], preferred_element_type=F32)

    @pl.when(f == 0)
    def _():
        acc_scr[...] = part

    @pl.when(f > 0)
    def _():
        acc_scr[...] += part

    @pl.when(f == pl.num_programs(2) - 1)
    def _():
        o_ref[...] = (acc_scr[...] * gate_ref[...]).astype(BF16)


def _expert_ffn(xe, gate, w_gate, w_up, w_down, layer, tm=1024, tf=512):
    ne, cap, d = xe.shape
    dff = w_gate.shape[-1]
    tm = min(tm, cap)
    return pl.pallas_call(
        _ffn_kernel,
        out_shape=jax.ShapeDtypeStruct((ne, cap, d), BF16),
        grid=(ne, cap // tm, dff // tf),
        in_specs=[
            pl.BlockSpec((None, tm, d), lambda e, i, f: (e, i, 0)),
            pl.BlockSpec((None, tm, 1), lambda e, i, f: (e, i, 0)),
            pl.BlockSpec((None, None, d, tf), lambda e, i, f: (layer, e, 0, f)),
            pl.BlockSpec((None, None, d, tf), lambda e, i, f: (layer, e, 0, f)),
            pl.BlockSpec((None, None, tf, d), lambda e, i, f: (layer, e, f, 0)),
        ],
        out_specs=pl.BlockSpec((None, tm, d), lambda e, i, f: (e, i, 0)),
        scratch_shapes=[pltpu.VMEM((tm, d), F32)],
        compiler_params=_params(("arbitrary", "arbitrary", "arbitrary"), 48),
        name="expert_ffn",
    )(xe, gate, w_gate, w_up, w_down)


WINDOW = 256


def _combine_copy(ye_hbm, buf, sem, expert, window, slot):
    start = pl.multiple_of(window * WINDOW, WINDOW)
    return pltpu.make_async_copy(ye_hbm.at[expert, pl.ds(start, WINDOW), :], buf.at[slot],
                                 sem.at[slot])


def _combine_kernel(cnt_ref, we_ref, ww_ref, idx_ref, x_ref, gt_ref, g_ref, sc_ref, sh_ref, ye_hbm,
                    o_ref, *rest, tm, final_norm):
    if final_norm:
        buf, sem, acc_scr = rest
    else:
        h_ref, buf, sem, acc_scr = rest
    tile = pl.program_id(0)
    n = cnt_ref[tile]
    tokens = tile * tm + lax.broadcasted_iota(jnp.int32, (tm, WINDOW), 0)
    acc_scr[...] = jnp.zeros_like(acc_scr)

    @pl.when(n > 0)
    def _():
        _combine_copy(ye_hbm, buf, sem, we_ref[tile, 0], ww_ref[tile, 0], 0).start()

    def body(p, _):
        slot = lax.rem(p, 2)
        expert = we_ref[tile, p]
        window = ww_ref[tile, p]
        _combine_copy(ye_hbm, buf, sem, expert, window, slot).wait()

        @pl.when(p + 1 < n)
        def _():
            _combine_copy(ye_hbm, buf, sem, we_ref[tile, p + 1], ww_ref[tile, p + 1],
                          1 - slot).start()

        ids = idx_ref[pl.ds(expert, 1), pl.ds(pl.multiple_of(window * WINDOW, WINDOW), WINDOW)]
        onehot = jnp.where(ids == tokens, 1.0, 0.0).astype(BF16)
        acc_scr[...] += jnp.dot(onehot, buf[slot], preferred_element_type=F32)
        return 0

    lax.fori_loop(0, n, body, 0)
    x = x_ref[...] + gt_ref[...] * acc_scr[...]
    if final_norm:
        ms = jnp.mean(x * x, axis=-1, keepdims=True)
        o_ref[...] = x * lax.rsqrt(ms + RMS_EPS) * g_ref[...]
    else:
        o_ref[...] = x
        h_ref[...] = _rms_mod(x, g_ref[...], sc_ref[...], sh_ref[...]).astype(BF16)


def _combine_plan(idx_sorted, n_tokens, tm):
    ne, cap = idx_sorted.shape
    nt = n_tokens // tm
    bounds = jnp.arange(nt + 1, dtype=jnp.int32) * tm
    base = jnp.sum(idx_sorted[:, :, None] < bounds[None, None, :], axis=1, dtype=jnp.int32)
    lo, hi = base[:, :-1], base[:, 1:]
    first = lo // WINDOW
    nwin = jnp.where(hi > lo, (hi - 1) // WINDOW - first + 1, 0)
    cum_incl = jnp.cumsum(nwin, axis=0)
    cum_excl = cum_incl - nwin
    count = cum_incl[-1]
    max_pairs = ne * (tm // WINDOW + 1)
    p = jnp.arange(max_pairs, dtype=jnp.int32)
    expert = jnp.sum(cum_incl.T[:, None, :] <= p[None, :, None], axis=-1, dtype=jnp.int32)
    expert = jnp.minimum(expert, ne - 1)
    k = p[None, :] - jnp.take_along_axis(cum_excl.T, expert, axis=1)
    window = jnp.take_along_axis(first.T, expert, axis=1) + k
    window = jnp.clip(window, 0, cap // WINDOW - 1)
    return count.astype(jnp.int32), expert, window.astype(jnp.int32)


def _combine(x, ye, idx_sorted, mod, g_next, layer, seq_len, final_norm, tm=512):
    n, d = x.shape
    tm = min(tm, seq_len)
    tiles_per_seq = seq_len // tm
    seq_of = lambda i, *_: i // tiles_per_seq
    count, expert, window = _combine_plan(idx_sorted, n, tm)
    kern = functools.partial(_combine_kernel, tm=tm, final_norm=final_norm)
    row = pl.BlockSpec((tm, d), lambda i, *_: (i, 0))
    modrow = lambda lyr, chunk: pl.BlockSpec((None, None, None, 1, d), _mod_spec(lyr, chunk, seq_of))
    if final_norm:
        g_spec = pl.BlockSpec((1, d), lambda i, *_: (0, 0))
        norm_specs = [g_spec, modrow(layer, 1), modrow(layer, 0)]
        out_shape = jax.ShapeDtypeStruct((n, d), F32)
        out_specs = row
    else:
        g_spec = pl.BlockSpec((None, 1, d), lambda i, *_: (layer + 1, 0, 0))
        norm_specs = [g_spec, modrow(layer + 1, 1), modrow(layer + 1, 0)]
        out_shape = (jax.ShapeDtypeStruct((n, d), F32), jax.ShapeDtypeStruct((n, d), BF16))
        out_specs = (row, row)
    grid_spec = pltpu.PrefetchScalarGridSpec(
        num_scalar_prefetch=3,
        grid=(n // tm,),
        in_specs=[
            pl.BlockSpec(idx_sorted.shape, lambda i, *_: (0, 0)),
            row,
            modrow(layer, N_MOD - 1),
            *norm_specs,
            pl.BlockSpec(memory_space=pl.ANY),
        ],
        out_specs=out_specs,
        scratch_shapes=[pltpu.VMEM((2, WINDOW, d), BF16), pltpu.SemaphoreType.DMA((2,)),
                        pltpu.VMEM((tm, d), F32)],
    )
    return pl.pallas_call(
        kern,
        out_shape=out_shape,
        grid_spec=grid_spec,
        compiler_params=_params(("arbitrary",), 40),
        name="combine",
    )(count, expert, window, idx_sorted, x, mod, g_next, mod, mod, ye)


def _trunk(x, c_rows, group_sizes, seq_len, g_norm1, g_norm2, w_ada, b_ada, w_in, w_branch_gate,
           w_br_a, w_br_b, w_br_c, w_o, lambda_q1, lambda_k1, lambda_q2, lambda_k2, g_subln, sink,
           w_router, w_e_gate, w_e_up, w_e_down, g_final):
    n, d = x.shape
    depth = w_in.shape[0]
    batch = n // seq_len
    s_a, s_b, s_c = _alibi_slopes()

    w_cat, col_scale = _proj_weights(w_in, w_branch_gate)
    wa, wb, wc, wo = (w.astype(BF16) for w in (w_br_a, w_br_b, w_br_c, w_o))
    weg, weu, wed = (w.astype(BF16) for w in (w_e_gate, w_e_up, w_e_down))
    w_router_t = jnp.swapaxes(w_router, 1, 2)
    row3 = lambda a: a.reshape(depth, 1, a.shape[-1])
    g1, g2 = row3(g_norm1), row3(g_norm2)
    lq1, lk1, lq2, lk2, gsub = (row3(a) for a in (lambda_q1, lambda_k1, lambda_q2, lambda_k2, g_subln))
    feat = _position_features(seq_len)

    mod = _ada_modulation(c_rows, w_ada, b_ada)
    mod = mod.reshape(depth, c_rows.shape[0], N_MOD, 1, d)

    h1 = _norm_mod(x, g1, mod, 0, seq_len)
    for layer in range(depth):
        *a_qkv, mg = _proj(h1, w_cat, col_scale, layer, batch, seq_len)
        oa_parts = [
            _dilated_attention(a_qkv[gi], gi,
                               s_a[gi * A_HEADS_PER_GROUP:(gi + 1) * A_HEADS_PER_GROUP],
                               batch, seq_len)
            for gi in range(len(A_GROUPS))
        ]
        ob = _diff_attention(mg, feat, jnp.asarray(s_b), lq1, lk1, lq2, lk2, gsub, layer, batch,
                             seq_len)
        oc = _window_attention(mg, jnp.asarray(s_c), sink[layer], batch, seq_len)
        x, h2, aff_t = _merge(x, mod, oa_parts, ob, oc, mg, wa, wb, wc, wo, g2, w_router_t, layer,
                              seq_len)

        idx_parts, gate_parts = [], []
        start = 0
        for size in group_sizes:
            cap = (EC_CAPACITY_FACTOR * size) // N_EXPERTS
            gate, idx = lax.top_k(aff_t[:, start:start + size], cap)
            idx_parts.append(idx + start)
            gate_parts.append(gate)
            start += size
        idx = jnp.concatenate(idx_parts, axis=1)
        gate = jnp.concatenate(gate_parts, axis=1)
        idx, gate = lax.sort((idx, gate), dimension=1, num_keys=1)
        xe = jnp.take(h2, idx, axis=0)
        ye = _expert_ffn(xe, gate[..., None], weg, weu, wed, layer)
        if layer == depth - 1:
            x = _combine(x, ye, idx, mod, g_final.reshape(1, d), layer, seq_len, True)
        else:
            x, h1 = _combine(x, ye, idx, mod, g1, layer, seq_len, False)
    return x


def kernel(x_prompt, x_sample, c_prompt, c_sample, g_norm1, g_norm2, w_ada, b_ada, w_in, w_branch_gate, w_br_a, w_br_b, w_br_c, w_o, lambda_q1, lambda_k1, lambda_q2, lambda_k2, g_subln, sink, w_router, w_e_gate, w_e_up, w_e_down, g_final):
    bp, seq_len, d = x_prompt.shape
    bs = x_sample.shape[0]
    assert x_sample.shape[1] == seq_len
    x = jnp.concatenate([x_prompt.reshape(bp * seq_len, d), x_sample.reshape(bs * seq_len, d)], axis=0)
    c = jnp.concatenate([c_prompt, c_sample], axis=0)
    pad = -c.shape[0] % SUBLANES
    c_rows = jnp.pad(c, ((0, pad), (0, 0)))
    y = _trunk(x, c_rows, (bp * seq_len, bs * seq_len), seq_len, g_norm1, g_norm2, w_ada, b_ada,
               w_in, w_branch_gate, w_br_a, w_br_b, w_br_c, w_o, lambda_q1, lambda_k1, lambda_q2,
               lambda_k2, g_subln, sink, w_router, w_e_gate, w_e_up, w_e_down, g_final)
    y_prompt = y[:bp * seq_len].reshape(bp, seq_len, d)
    y_sample = y[bp * seq_len:].reshape(bs, seq_len, d)
    return (y_prompt, y_sample)
```

```python
import functools
import math

import numpy as np
import jax
import jax.numpy as jnp
from jax import lax
from jax.experimental import pallas as pl
from jax.experimental.pallas import tpu as pltpu

F32 = jnp.float32
BF16 = jnp.bfloat16

A_GROUPS = ((128, 1), (512, 4), (2048, 16))
A_HEADS_PER_GROUP = 4
A_HEAD_DIM = 64
A_N_HEADS = 12
A_OUT = 256
A_GROUP_COLS = A_HEADS_PER_GROUP * A_HEAD_DIM
A_TILE = 3 * A_GROUP_COLS
B_HEADS = 6
B_QK_DIM = 64
B_V_DIM = 128
B_OUT = 768
C_Q_HEADS = 8
C_KV_HEADS = 2
C_GROUP = C_Q_HEADS // C_KV_HEADS
C_HEAD_DIM = 128
C_HALF_WINDOW = 128
C_OUT = 1024
A_COLS = 2304
B_COLS = 2304
C_COLS = 1536
IN_COLS = 6144
N_BRANCH = 3
N_EXPERTS = 16
EC_CAPACITY_FACTOR = 2
N_ALIBI_HEADS = 26
RMS_EPS = 1e-6
NEG_INF = -1e30
LOG2E = math.log2(math.e)
N_MOD = 6
SUBLANES = 8
LANES = 128
POS_SPLIT = 64
N_PIECES = 3

_NT_DIMS = (((1,), (1,)), ((), ()))
_MIB = 1024 * 1024


def _alibi_slopes():
    n = N_ALIBI_HEADS
    s = 2.0 ** (-8.0 * np.arange(1, n + 1, dtype=np.float32) / n)
    s = s.astype(np.float32)
    s_c = s[:C_Q_HEADS]
    s_a = s[C_Q_HEADS:C_Q_HEADS + A_N_HEADS]
    s_b = s[C_Q_HEADS + A_N_HEADS:]
    return s_a, s_b, s_c


def _params(semantics, vmem_mib):
    return pltpu.CompilerParams(dimension_semantics=semantics,
                                vmem_limit_bytes=vmem_mib * _MIB)


def _mod_spec(layer, chunk, seq_of):
    def index(*ids):
        return (layer, seq_of(*ids), chunk, 0, 0)
    return index


def _rms_mod(x, g, scale, shift):
    ms = jnp.mean(x * x, axis=-1, keepdims=True)
    y = x * lax.rsqrt(ms + RMS_EPS) * g
    return y * (1.0 + scale) + shift


def _split3(x):
    hi = x.astype(BF16).astype(F32)
    mid = (x - hi).astype(BF16).astype(F32)
    lo = (x - hi - mid).astype(BF16).astype(F32)
    return hi, mid, lo


def _ada_kernel(c_ref, w_ref, b_ref, o_ref):
    c = c_ref[...]
    a = c * jax.nn.sigmoid(c)
    o_ref[...] = jnp.dot(a, w_ref[...], precision=lax.Precision.HIGHEST,
                         preferred_element_type=F32) + b_ref[...]


def _ada_modulation(c_rows, w_ada, b_ada, tn=1024):
    depth, d, n6 = w_ada.shape
    rows = c_rows.shape[0]
    return pl.pallas_call(
        _ada_kernel,
        out_shape=jax.ShapeDtypeStruct((depth, rows, n6), F32),
        grid=(depth, n6 // tn),
        in_specs=[
            pl.BlockSpec((rows, d), lambda l, j: (0, 0)),
            pl.BlockSpec((None, d, tn), lambda l, j: (l, 0, j)),
            pl.BlockSpec((None, 1, tn), lambda l, j: (l, 0, j)),
        ],
        out_specs=pl.BlockSpec((None, rows, tn), lambda l, j: (l, 0, j)),
        compiler_params=_params(("arbitrary", "arbitrary"), 40),
        name="ada_modulation",
    )(c_rows, w_ada, b_ada.reshape(depth, 1, n6))


def _norm_kernel(x_ref, g_ref, sc_ref, sh_ref, h_ref):
    h_ref[...] = _rms_mod(x_ref[...], g_ref[...], sc_ref[...], sh_ref[...]).astype(BF16)


def _norm_mod(x, g, mod, layer, seq_len, tm=512):
    n, d = x.shape
    tm = min(tm, seq_len)
    tps = seq_len // tm
    seq_of = lambda i: i // tps
    row = pl.BlockSpec((tm, d), lambda i: (i, 0))
    return pl.pallas_call(
        _norm_kernel,
        out_shape=jax.ShapeDtypeStruct((n, d), BF16),
        grid=(n // tm,),
        in_specs=[row,
                  pl.BlockSpec((None, 1, d), lambda i: (layer, 0, 0)),
                  pl.BlockSpec((None, None, None, 1, d), _mod_spec(layer, 1, seq_of)),
                  pl.BlockSpec((None, None, None, 1, d), _mod_spec(layer, 0, seq_of))],
        out_specs=row,
        compiler_params=_params(("arbitrary",), 32),
        name="norm_mod",
    )(x, g, mod, mod)


PROJ_TILE = A_TILE
N_A_TILES = len(A_GROUPS)
N_GATE_TILES = IN_COLS // PROJ_TILE
N_MIX_TILES = (C_COLS + B_COLS) // PROJ_TILE
MG_C_Q = IN_COLS
MG_C_K = MG_C_Q + C_OUT
MG_C_V = MG_C_K + C_KV_HEADS * C_HEAD_DIM
MG_B_Q = MG_C_Q + C_COLS
MG_B_K = MG_B_Q + B_OUT
MG_B_V = MG_B_K + B_OUT


def _proj_kernel(h_ref, w_ref, cs_ref, a0_ref, a1_ref, a2_ref, mg_ref, acc_scr, *, tm):
    j = pl.program_id(1)
    acc = jnp.dot(h_ref[...], w_ref[...], preferred_element_type=F32) * cs_ref[...]

    @pl.when(j == 0)
    def _():
        a0_ref[0] = acc.astype(BF16)

    for gi, a_ref in ((1, a1_ref), (2, a2_ref)):
        dil = A_GROUPS[gi][1]

        @pl.when(j == gi)
        def _(a_ref=a_ref, dil=dil):
            nslab = PROJ_TILE // LANES
            for c in range(nslab):
                acc_scr[c] = acc[:, c * LANES:(c + 1) * LANES]
            for r in range(dil):
                rows = [acc_scr[c, pl.ds(r, tm // dil, stride=dil), :] for c in range(nslab)]
                a_ref[r] = jnp.concatenate(rows, axis=1).astype(BF16)

    is_gate = (j >= N_A_TILES) & (j < N_A_TILES + N_GATE_TILES)
    mg_ref[...] = jnp.where(is_gate, 0.5 * jnp.tanh(0.5 * acc) + 0.5, acc).astype(BF16)


def _proj(h, w_cat, col_scale, layer, batch, seq_len, tm=1024):
    n, d = h.shape
    tn = PROJ_TILE
    tm = min(tm, seq_len)
    tps = seq_len // tm
    a_shapes, a_specs = [], []
    for _, dil in A_GROUPS:
        a_shapes.append(jax.ShapeDtypeStruct((batch, dil, seq_len // dil, tn), BF16))
        a_specs.append(pl.BlockSpec((None, dil, tm // dil, tn), lambda i, j: (i // tps, 0, i % tps, 0)))
    n_mg = N_GATE_TILES + N_MIX_TILES
    kern = functools.partial(_proj_kernel, tm=tm)
    return pl.pallas_call(
        kern,
        out_shape=(*a_shapes, jax.ShapeDtypeStruct((n, n_mg * tn), BF16)),
        grid=(n // tm, N_A_TILES + n_mg),
        in_specs=[
            pl.BlockSpec((tm, d), lambda i, j: (i, 0)),
            pl.BlockSpec((None, d, tn), lambda i, j: (layer, 0, j)),
            pl.BlockSpec((1, tn), lambda i, j: (0, j)),
        ],
        out_specs=(*a_specs,
                   pl.BlockSpec((tm, tn), lambda i, j: (i, jnp.clip(j - N_A_TILES, 0, n_mg - 1)))),
        scratch_shapes=[pltpu.VMEM((tn // LANES, tm, LANES), F32)],
        compiler_params=_params(("arbitrary", "arbitrary"), 48),
        name="proj",
    )(h, w_cat, col_scale)


def _proj_weights(w_in, w_branch_gate):
    cols = []
    for gi in range(len(A_GROUPS)):
        for part in range(3):
            start = part * (A_COLS // 3) + gi * A_GROUP_COLS
            cols.append(np.arange(start, start + A_GROUP_COLS))
    perm_a = np.concatenate(cols)
    c0 = A_COLS + B_COLS
    w_cat = jnp.concatenate([w_in[..., perm_a], w_branch_gate, w_in[..., c0:c0 + C_COLS],
                             w_in[..., A_COLS:A_COLS + B_COLS]], axis=-1).astype(BF16)

    scale = np.ones((2 * IN_COLS,), np.float32)
    for gi in range(len(A_GROUPS)):
        scale[gi * A_TILE:gi * A_TILE + A_GROUP_COLS] = A_HEAD_DIM ** -0.5 * LOG2E
    qc = A_COLS + MG_C_Q
    scale[qc:qc + C_OUT] = C_HEAD_DIM ** -0.5 * LOG2E
    qb = A_COLS + MG_B_Q
    scale[qb:qb + B_OUT] = B_QK_DIM ** -0.5 * LOG2E
    return w_cat, jnp.asarray(scale).reshape(1, -1)


def _position_features(seq_len):
    j = np.arange(seq_len)
    feat = np.zeros((seq_len, LANES), np.float32)
    feat[:, 0:N_PIECES] = ((j // POS_SPLIT) * POS_SPLIT)[:, None]
    feat[:, N_PIECES:2 * N_PIECES] = (j % POS_SPLIT)[:, None]
    feat[:, 2 * N_PIECES:3 * N_PIECES] = 1.0
    return jnp.asarray(feat, BF16)


ONES_COL = B_V_DIM + 2 * N_PIECES


def _battn_kernel(slope_ref, q_ref, k_ref, v_ref, feat_ref, lq1_ref, lk1_ref, lq2_ref, lk2_ref,
                  gs_ref, o_ref, *, tq, tk, seq_len, lam_init):
    h = pl.program_id(1)
    qi = pl.program_id(2)
    slope = slope_ref[h] * LOG2E
    q = q_ref[...]
    lane = lax.broadcasted_iota(jnp.int32, (tq, LANES), 1)
    zero = jnp.zeros_like(q)
    qq = jnp.concatenate([jnp.where(lane < B_QK_DIM, q, zero),
                          jnp.where(lane >= B_QK_DIM, q, zero)], axis=0)
    row = lax.broadcasted_iota(jnp.int32, (2 * tq, 1), 0)
    qpos = qi * tq + jnp.where(row >= tq, row - tq, row)

    lane2 = lax.broadcasted_iota(jnp.int32, (2 * tq, LANES), 1)
    piece = lane2 % N_PIECES
    s_hi, s_mid, s_lo = _split3(jnp.full((1, LANES), slope, F32))
    c_hi, c_mid, c_lo = _split3(slope * qpos.astype(F32))
    s_piece = jnp.where(piece == 0, s_hi, jnp.where(piece == 1, s_mid, s_lo))
    c_piece = jnp.where(piece == 0, c_hi, jnp.where(piece == 1, c_mid, c_lo))
    aug = jnp.where(lane2 < 2 * N_PIECES, -s_piece, jnp.where(lane2 < 3 * N_PIECES, c_piece, 0.0))
    lhs_right = jnp.concatenate([qq, aug.astype(BF16)], axis=1)
    lhs_left = jnp.concatenate([qq, (-aug).astype(BF16)], axis=1)

    nchunks = seq_len // tk
    jd = (qi * tq) // tk

    def chunk(t, carry, diag):
        m, acc = carry
        j = lax.rem(jd + t, nchunks)
        k0 = pl.multiple_of(j * tk, tk)
        kc = k_ref[pl.ds(k0, tk), :]
        fc = feat_ref[pl.ds(k0, tk), :]
        vf = jnp.concatenate([v_ref[pl.ds(k0, tk), :], fc], axis=1)
        if diag:
            s = lax.dot_general(qq, kc, _NT_DIMS, preferred_element_type=F32)
            kpos = k0 + lax.broadcasted_iota(jnp.int32, (1, tk), 1)
            s = s - slope * jnp.abs(qpos - kpos).astype(F32)
        else:
            lhs = jnp.where(j < jd, lhs_left, lhs_right)
            s = lax.dot_general(lhs, jnp.concatenate([kc, fc], axis=1), _NT_DIMS,
                                preferred_element_type=F32)
        m_new = jnp.maximum(m, jnp.max(s, axis=-1, keepdims=True))
        alpha = jnp.exp2(m - m_new)
        p = jnp.exp2(s - m_new).astype(BF16)
        acc = alpha * acc + jnp.dot(p, vf, preferred_element_type=F32)
        return m_new, acc

    carry = (jnp.full((2 * tq, 1), NEG_INF, F32), jnp.zeros((2 * tq, 2 * LANES), F32))
    carry = chunk(0, carry, True)
    for t in range(1, nchunks):
        carry = chunk(t, carry, False)
    _, acc = carry

    o = acc[:, :B_V_DIM] / acc[:, ONES_COL:ONES_COL + 1]
    lam = (jnp.exp(jnp.sum(lq1_ref[...] * lk1_ref[...], axis=-1, keepdims=True))
           - jnp.exp(jnp.sum(lq2_ref[...] * lk2_ref[...], axis=-1, keepdims=True)) + lam_init)
    diff = o[:tq] - lam * o[tq:]
    ms = jnp.mean(diff * diff, axis=-1, keepdims=True)
    y = diff * lax.rsqrt(ms + RMS_EPS) * gs_ref[...]
    o_ref[...] = (y * (1.0 - lam_init)).astype(BF16)


def _diff_attention(mix, feat, slopes_b, lq1, lk1, lq2, lk2, g_subln, layer, batch, seq_len,
                    tq=512, tk=1024):
    n = mix.shape[0]
    tq = min(tq, seq_len)
    tk = min(tk, seq_len)
    nq = seq_len // tq
    qcol, kcol, vcol = (c // B_V_DIM for c in (MG_B_Q, MG_B_K, MG_B_V))
    lam_init = 0.8 - 0.6 * math.exp(-0.3 * layer)
    kern = functools.partial(_battn_kernel, tq=tq, tk=tk, seq_len=seq_len, lam_init=lam_init)
    vec = lambda width: pl.BlockSpec((None, 1, width), lambda b, h, i: (layer, 0, 0))
    return pl.pallas_call(
        kern,
        out_shape=jax.ShapeDtypeStruct((n, B_OUT), BF16),
        grid=(batch, B_HEADS, nq),
        in_specs=[
            pl.BlockSpec(memory_space=pltpu.SMEM),
            pl.BlockSpec((tq, B_V_DIM), lambda b, h, i: (b * nq + i, qcol + h)),
            pl.BlockSpec((seq_len, B_V_DIM), lambda b, h, i: (b, kcol + h)),
            pl.BlockSpec((seq_len, B_V_DIM), lambda b, h, i: (b, vcol + h)),
            pl.BlockSpec((seq_len, LANES), lambda b, h, i: (0, 0)),
            vec(B_QK_DIM), vec(B_QK_DIM), vec(B_QK_DIM), vec(B_QK_DIM), vec(B_V_DIM),
        ],
        out_specs=pl.BlockSpec((tq, B_V_DIM), lambda b, h, i: (b * nq + i, h)),
        compiler_params=_params(("arbitrary", "arbitrary", "arbitrary"), 48),
        name="diff_attention",
    )(slopes_b, mix, mix, mix, feat, lq1, lk1, lq2, lk2, g_subln)


def _band_window(qi, tq, half, length):
    kw = tq + 2 * half
    ks = jnp.clip(qi * tq - half, 0, length - kw)
    ks = pl.multiple_of(ks, half)
    qpos = qi * tq + lax.broadcasted_iota(jnp.int32, (tq, 1), 0)
    kpos = ks + lax.broadcasted_iota(jnp.int32, (1, kw), 1)
    rel = jnp.abs(qpos - kpos)
    return ks, kw, rel <= half, rel.astype(F32)


def _cattn_kernel(slope_ref, sink_ref, q_ref, k_ref, v_ref, o_ref, *, tq, seq_len):
    g = pl.program_id(1)
    qi = pl.program_id(2)
    ks, kw, valid, relf = _band_window(qi, tq, C_HALF_WINDOW, seq_len)
    kwin = k_ref[pl.ds(ks, kw), :]
    vwin = v_ref[pl.ds(ks, kw), :]
    for hh in range(C_GROUP):
        head = g * C_GROUP + hh
        slope = slope_ref[head] * LOG2E
        sink = sink_ref[head] * LOG2E
        cols = slice(hh * C_HEAD_DIM, (hh + 1) * C_HEAD_DIM)
        s = lax.dot_general(q_ref[:, cols], kwin, _NT_DIMS, preferred_element_type=F32)
        s = jnp.where(valid, s - slope * relf, NEG_INF)
        m = jnp.maximum(jnp.max(s, axis=-1, keepdims=True), sink)
        p = jnp.exp2(s - m)
        den = jnp.sum(p, axis=-1, keepdims=True) + jnp.exp2(sink - m)
        o = jnp.dot(p.astype(BF16), vwin, preferred_element_type=F32) / den
        o_ref[:, cols] = o.astype(BF16)


def _window_attention(mix, slopes_c, sink, batch, seq_len, tq=256):
    n = mix.shape[0]
    nq = seq_len // tq
    gw = C_GROUP * C_HEAD_DIM
    qcol = MG_C_Q // gw
    kcol = MG_C_K // C_HEAD_DIM
    vcol = MG_C_V // C_HEAD_DIM
    kern = functools.partial(_cattn_kernel, tq=tq, seq_len=seq_len)
    return pl.pallas_call(
        kern,
        out_shape=jax.ShapeDtypeStruct((n, C_OUT), BF16),
        grid=(batch, C_KV_HEADS, nq),
        in_specs=[
            pl.BlockSpec(memory_space=pltpu.SMEM),
            pl.BlockSpec(memory_space=pltpu.SMEM),
            pl.BlockSpec((tq, gw), lambda b, g, i: (b * nq + i, qcol + g)),
            pl.BlockSpec((seq_len, C_HEAD_DIM), lambda b, g, i: (b, kcol + g)),
            pl.BlockSpec((seq_len, C_HEAD_DIM), lambda b, g, i: (b, vcol + g)),
        ],
        out_specs=pl.BlockSpec((tq, gw), lambda b, g, i: (b * nq + i, g)),
        compiler_params=_params(("arbitrary", "arbitrary", "arbitrary"), 32),
        name="window_attention",
    )(slopes_c, sink, mix, mix, mix)


def _aattn_kernel(q_ref, k_ref, v_ref, o_ref, lse_ref, *scratch, tq, length, slopes, dil):
    qi = pl.program_id(1)
    nslab = A_GROUP_COLS // LANES
    half = A_GROUPS[0][0] // 2
    ks, kw, valid, relf = _band_window(qi, tq, half, length)
    lane = lax.broadcasted_iota(jnp.int32, (tq, A_GROUP_COLS), 1)

    def residue(r, _):
        kwin = k_ref[r, pl.ds(ks, kw), :]
        vwin = v_ref[r, pl.ds(ks, kw), :]
        q = q_ref[r]
        zero = jnp.zeros_like(q)
        out = jnp.zeros((tq, A_GROUP_COLS), F32)
        lse_out = jnp.zeros((tq, A_GROUP_COLS), F32)
        for hh in range(A_HEADS_PER_GROUP):
            in_head = (lane >= hh * A_HEAD_DIM) & (lane < (hh + 1) * A_HEAD_DIM)
            s = lax.dot_general(jnp.where(in_head, q, zero), kwin, _NT_DIMS,
                                preferred_element_type=F32)
            s = jnp.where(valid, s - (slopes[hh] * dil * LOG2E) * relf, NEG_INF)
            m = jnp.max(s, axis=-1, keepdims=True)
            p = jnp.exp2(s - m)
            den = jnp.sum(p, axis=-1, keepdims=True)
            o = jnp.dot(p.astype(BF16), vwin, preferred_element_type=F32) / den
            out = jnp.where(in_head, o, out)
            lse_out = jnp.where(in_head, m + jnp.log2(den), lse_out)
        if dil == 1:
            o_ref[...] = out
            lse_ref[...] = lse_out
        else:
            o_scr, lse_scr = scratch
            for c in range(nslab):
                cols = slice(c * LANES, (c + 1) * LANES)
                o_scr[c, pl.ds(r, tq, stride=dil), :] = out[:, cols]
                lse_scr[c, pl.ds(r, tq, stride=dil), :] = lse_out[:, cols]
        return 0

    if dil == 1:
        residue(0, 0)
    else:
        lax.fori_loop(0, dil, residue, 0)
        o_scr, lse_scr = scratch
        o_ref[...] = jnp.concatenate([o_scr[c] for c in range(nslab)], axis=1)
        lse_ref[...] = jnp.concatenate([lse_scr[c] for c in range(nslab)], axis=1)


def _dilated_attention(qkv, group, slopes, batch, seq_len, tq=256):
    window, dil = A_GROUPS[group]
    half = window // (2 * dil)
    length = seq_len // dil
    tq = min(tq, length - 2 * half)
    nq = length // tq
    n = batch * seq_len
    kern = functools.partial(_aattn_kernel, tq=tq, length=length,
                             slopes=tuple(float(v) for v in slopes), dil=dil)
    out_sds = jax.ShapeDtypeStruct((n, A_GROUP_COLS), F32)
    out_spec = pl.BlockSpec((tq * dil, A_GROUP_COLS), lambda b, i: (b * nq + i, 0))
    kv_spec = lambda part: pl.BlockSpec((None, dil, length, A_GROUP_COLS), lambda b, i: (b, 0, 0, part),
                                        pipeline_mode=pl.Buffered(1))
    return pl.pallas_call(
        kern,
        out_shape=(out_sds, out_sds),
        grid=(batch, nq),
        in_specs=[
            pl.BlockSpec((None, dil, tq, A_GROUP_COLS), lambda b, i: (b, 0, i, 0)),
            kv_spec(1), kv_spec(2),
        ],
        out_specs=(out_spec, out_spec),
        scratch_shapes=[] if dil == 1 else
        [pltpu.VMEM((A_GROUP_COLS // LANES, tq * dil, LANES), F32)] * 2,
        compiler_params=_params(("arbitrary", "arbitrary"), 48),
        name=f"dilated_attention_g{group}",
    )(qkv, qkv, qkv)


def _merge_kernel(x_ref, gt_ref, o0_ref, o1_ref, o2_ref, l0_ref, l1_ref, l2_ref, ob_ref, oc_ref,
                  ga_ref, gb_ref, gc_ref, wa_ref, wb_ref, wc_ref, wo_ref, g2_ref, sc2_ref, sh2_ref,
                  wr_ref, out_ref, h_ref, aff_ref):
    l0, l1, l2 = l0_ref[...], l1_ref[...], l2_ref[...]
    m = jnp.maximum(jnp.maximum(l0, l1), l2)
    e0, e1, e2 = jnp.exp2(l0 - m), jnp.exp2(l1 - m), jnp.exp2(l2 - m)
    oa = (e0 * o0_ref[...] + e1 * o1_ref[...] + e2 * o2_ref[...]) / (e0 + e1 + e2)
    ya = jnp.dot(oa.astype(BF16), wa_ref[...], preferred_element_type=F32)
    yb = jnp.dot(ob_ref[...], wb_ref[...], preferred_element_type=F32)
    yc = jnp.dot(oc_ref[...], wc_ref[...], preferred_element_type=F32)
    merged = (ga_ref[...].astype(F32) * ya + gb_ref[...].astype(F32) * yb
              + gc_ref[...].astype(F32) * yc)
    y = jnp.dot(merged.astype(BF16), wo_ref[...], preferred_element_type=F32)
    x = x_ref[...] + gt_ref[...] * y
    out_ref[...] = x

    h = _rms_mod(x, g2_ref[...], sc2_ref[...], sh2_ref[...])
    h_ref[...] = h.astype(BF16)
    w = wr_ref[...]
    h_hi = h.astype(BF16)
    h_lo = (h - h_hi.astype(F32)).astype(BF16)
    w_hi = w.astype(BF16)
    w_lo = (w - w_hi.astype(F32)).astype(BF16)
    logits = (jnp.dot(h_hi, w_hi, preferred_element_type=F32)
              + jnp.dot(h_lo, w_hi, preferred_element_type=F32)
              + jnp.dot(h_hi, w_lo, preferred_element_type=F32))
    mx = jnp.max(logits, axis=-1, keepdims=True)
    e = jnp.exp(logits - mx)
    aff_ref[...] = e / jnp.sum(e, axis=-1, keepdims=True)


def _merge(x, mod, oa_parts, ob, oc, mg, w_br_a, w_br_b, w_br_c, w_o, g2, w_router, layer,
           seq_len, tm=256):
    n, d = x.shape
    tm = min(tm, seq_len)
    tiles_per_seq = seq_len // tm
    seq_of = lambda i: i // tiles_per_seq
    ne = w_router.shape[-1]
    (o0, l0), (o1, l1), (o2, l2) = oa_parts
    row = lambda width: pl.BlockSpec((tm, width), lambda i: (i, 0))
    resident = lambda rows: pl.BlockSpec((None, rows, d), lambda i: (layer, 0, 0),
                                         pipeline_mode=pl.Buffered(1))
    gate = lambda k: pl.BlockSpec((tm, d), lambda i: (i, k))
    modrow = lambda chunk: pl.BlockSpec((None, None, None, 1, d), _mod_spec(layer, chunk, seq_of))
    return pl.pallas_call(
        _merge_kernel,
        out_shape=(jax.ShapeDtypeStruct((n, d), F32), jax.ShapeDtypeStruct((n, d), BF16),
                   jax.ShapeDtypeStruct((n, ne), F32)),
        grid=(n // tm,),
        in_specs=[
            row(d), modrow(2),
            row(A_OUT), row(A_OUT), row(A_OUT), row(A_OUT), row(A_OUT), row(A_OUT),
            row(B_OUT), row(C_OUT),
            gate(0), gate(1), gate(2),
            resident(A_OUT), resident(B_OUT), resident(C_OUT), resident(d),
            pl.BlockSpec((None, 1, d), lambda i: (layer, 0, 0)), modrow(4), modrow(3),
            pl.BlockSpec((None, d, ne), lambda i: (layer, 0, 0)),
        ],
        out_specs=(row(d), row(d), row(ne)),
        compiler_params=_params(("arbitrary",), 56),
        name="merge",
    )(x, mod, o0, o1, o2, l0, l1, l2, ob, oc, mg, mg, mg, w_br_a, w_br_b, w_br_c, w_o,
      g2, mod, mod, w_router)


def _ffn_kernel(x_ref, gate_ref, wg_ref, wu_ref, wd_ref, o_ref, acc_scr):
    f = pl.program_id(2)

    @pl.when(f == 0)
    def _():
        acc_scr[...] = jnp.zeros_like(acc_scr)

    x = x_ref[...]
    a = jnp.dot(x, wg_ref[...], preferred_element_type=F32)
    u = jnp.dot(x, wu_ref[...], preferred_element_type=F32)
    hid = (a * jax.nn.sigmoid(a) * u).astype(BF16)
    acc_scr[...] += jnp.dot(hid, wd_ref[...], preferred_element_type=F32)

    @pl.when(f == pl.num_programs(2) - 1)
    def _():
        o_ref[...] = (acc_scr[...] * gate_ref[...]).astype(BF16)


def _expert_ffn(xe, gate, w_gate, w_up, w_down, layer, tm=1024, tf=512):
    ne, cap, d = xe.shape
    dff = w_gate.shape[-1]
    tm = min(tm, cap)
    return pl.pallas_call(
        _ffn_kernel,
        out_shape=jax.ShapeDtypeStruct((ne, cap, d), BF16),
        grid=(ne, cap // tm, dff // tf),
        in_specs=[
            pl.BlockSpec((None, tm, d), lambda e, i, f: (e, i, 0)),
            pl.BlockSpec((None, tm, 1), lambda e, i, f: (e, i, 0)),
            pl.BlockSpec((None, None, d, tf), lambda e, i, f: (layer, e, 0, f)),
            pl.BlockSpec((None, None, d, tf), lambda e, i, f: (layer, e, 0, f)),
            pl.BlockSpec((None, None, tf, d), lambda e, i, f: (layer, e, f, 0)),
        ],
        out_specs=pl.BlockSpec((None, tm, d), lambda e, i, f: (e, i, 0)),
        scratch_shapes=[pltpu.VMEM((tm, d), F32)],
        compiler_params=_params(("arbitrary", "arbitrary", "arbitrary"), 48),
        name="expert_ffn",
    )(xe, gate, w_gate, w_up, w_down)


WINDOW = 256
N_WINDOW_BUFS = 4


def _combine_copy(ye_hbm, buf, sem, expert, window, slot):
    start = pl.multiple_of(window * WINDOW, WINDOW)
    return pltpu.make_async_copy(ye_hbm.at[expert, pl.ds(start, WINDOW), :], buf.at[slot],
                                 sem.at[slot])


def _combine_kernel(cnt_ref, we_ref, ww_ref, idx_ref, x_ref, gt_ref, g_ref, sc_ref, sh_ref, ye_hbm,
                    o_ref, *rest, tm, final_norm):
    if final_norm:
        buf, sem, acc_scr = rest
    else:
        h_ref, buf, sem, acc_scr = rest
    tile = pl.program_id(0)
    n = cnt_ref[tile]
    tokens = tile * tm + lax.broadcasted_iota(jnp.int32, (tm, WINDOW), 0)
    acc_scr[...] = jnp.zeros_like(acc_scr)

    def start(p):
        _combine_copy(ye_hbm, buf, sem, we_ref[tile, p], ww_ref[tile, p],
                      lax.rem(p, N_WINDOW_BUFS)).start()

    for ahead in range(N_WINDOW_BUFS - 1):
        @pl.when(ahead < n)
        def _(ahead=ahead):
            start(ahead)

    def body(p, _):
        slot = lax.rem(p, N_WINDOW_BUFS)
        expert = we_ref[tile, p]
        window = ww_ref[tile, p]
        _combine_copy(ye_hbm, buf, sem, expert, window, slot).wait()

        @pl.when(p + N_WINDOW_BUFS - 1 < n)
        def _():
            start(p + N_WINDOW_BUFS - 1)

        ids = idx_ref[pl.ds(expert, 1), pl.ds(pl.multiple_of(window * WINDOW, WINDOW), WINDOW)]
        onehot = jnp.where(ids == tokens, 1.0, 0.0).astype(BF16)
        acc_scr[...] += jnp.dot(onehot, buf[slot], preferred_element_type=F32)
        return 0

    lax.fori_loop(0, n, body, 0)
    x = x_ref[...] + gt_ref[...] * acc_scr[...]
    if final_norm:
        ms = jnp.mean(x * x, axis=-1, keepdims=True)
        o_ref[...] = x * lax.rsqrt(ms + RMS_EPS) * g_ref[...]
    else:
        o_ref[...] = x
        h_ref[...] = _rms_mod(x, g_ref[...], sc_ref[...], sh_ref[...]).astype(BF16)


def _combine_plan(idx_sorted, n_tokens, tm):
    ne, cap = idx_sorted.shape
    nt = n_tokens // tm
    bounds = jnp.arange(nt + 1, dtype=jnp.int32) * tm
    base = jnp.sum(idx_sorted[:, :, None] < bounds[None, None, :], axis=1, dtype=jnp.int32)
    lo, hi = base[:, :-1], base[:, 1:]
    first = lo // WINDOW
    nwin = jnp.where(hi > lo, (hi - 1) // WINDOW - first + 1, 0)
    cum_incl = jnp.cumsum(nwin, axis=0)
    cum_excl = cum_incl - nwin
    count = cum_incl[-1]
    max_pairs = ne * (tm // WINDOW + 1)
    p = jnp.arange(max_pairs, dtype=jnp.int32)
    expert = jnp.sum(cum_incl.T[:, None, :] <= p[None, :, None], axis=-1, dtype=jnp.int32)
    expert = jnp.minimum(expert, ne - 1)
    k = p[None, :] - jnp.take_along_axis(cum_excl.T, expert, axis=1)
    window = jnp.take_along_axis(first.T, expert, axis=1) + k
    window = jnp.clip(window, 0, cap // WINDOW - 1)
    return count.astype(jnp.int32), expert, window.astype(jnp.int32)


def _combine(x, ye, idx_sorted, mod, g_next, layer, seq_len, final_norm, tm=512):
    n, d = x.shape
    tm = min(tm, seq_len)
    tiles_per_seq = seq_len // tm
    seq_of = lambda i, *_: i // tiles_per_seq
    count, expert, window = _combine_plan(idx_sorted, n, tm)
    kern = functools.partial(_combine_kernel, tm=tm, final_norm=final_norm)
    row = pl.BlockSpec((tm, d), lambda i, *_: (i, 0))
    modrow = lambda lyr, chunk: pl.BlockSpec((None, None, None, 1, d), _mod_spec(lyr, chunk, seq_of))
    if final_norm:
        g_spec = pl.BlockSpec((1, d), lambda i, *_: (0, 0))
        norm_specs = [g_spec, modrow(layer, 1), modrow(layer, 0)]
        out_shape = jax.ShapeDtypeStruct((n, d), F32)
        out_specs = row
    else:
        g_spec = pl.BlockSpec((None, 1, d), lambda i, *_: (layer + 1, 0, 0))
        norm_specs = [g_spec, modrow(layer + 1, 1), modrow(layer + 1, 0)]
        out_shape = (jax.ShapeDtypeStruct((n, d), F32), jax.ShapeDtypeStruct((n, d), BF16))
        out_specs = (row, row)
    grid_spec = pltpu.PrefetchScalarGridSpec(
        num_scalar_prefetch=3,
        grid=(n // tm,),
        in_specs=[
            pl.BlockSpec(idx_sorted.shape, lambda i, *_: (0, 0)),
            row,
            modrow(layer, N_MOD - 1),
            *norm_specs,
            pl.BlockSpec(memory_space=pl.ANY),
        ],
        out_specs=out_specs,
        scratch_shapes=[pltpu.VMEM((N_WINDOW_BUFS, WINDOW, d), BF16),
                        pltpu.SemaphoreType.DMA((N_WINDOW_BUFS,)),
                        pltpu.VMEM((tm, d), F32)],
    )
    return pl.pallas_call(
        kern,
        out_shape=out_shape,
        grid_spec=grid_spec,
        compiler_params=_params(("arbitrary",), 40),
        name="combine",
    )(count, expert, window, idx_sorted, x, mod, g_next, mod, mod, ye)


def _trunk(x, c_rows, group_sizes, seq_len, g_norm1, g_norm2, w_ada, b_ada, w_in, w_branch_gate,
           w_br_a, w_br_b, w_br_c, w_o, lambda_q1, lambda_k1, lambda_q2, lambda_k2, g_subln, sink,
           w_router, w_e_gate, w_e_up, w_e_down, g_final):
    n, d = x.shape
    depth = w_in.shape[0]
    batch = n // seq_len
    s_a, s_b, s_c = _alibi_slopes()

    w_cat, col_scale = _proj_weights(w_in, w_branch_gate)
    wa, wb, wc, wo = (w.astype(BF16) for w in (w_br_a, w_br_b, w_br_c, w_o))
    weg, weu, wed = (w.astype(BF16) for w in (w_e_gate, w_e_up, w_e_down))
    row3 = lambda a: a.reshape(depth, 1, a.shape[-1])
    g1, g2 = row3(g_norm1), row3(g_norm2)
    lq1, lk1, lq2, lk2, gsub = (row3(a) for a in (lambda_q1, lambda_k1, lambda_q2, lambda_k2, g_subln))
    feat = _position_features(seq_len)

    mod = _ada_modulation(c_rows, w_ada, b_ada)
    mod = mod.reshape(depth, c_rows.shape[0], N_MOD, 1, d)

    h1 = _norm_mod(x, g1, mod, 0, seq_len)
    for layer in range(depth):
        *a_qkv, mg = _proj(h1, w_cat, col_scale, layer, batch, seq_len)
        oa_parts = [
            _dilated_attention(a_qkv[gi], gi,
                               s_a[gi * A_HEADS_PER_GROUP:(gi + 1) * A_HEADS_PER_GROUP],
                               batch, seq_len)
            for gi in range(len(A_GROUPS))
        ]
        ob = _diff_attention(mg, feat, jnp.asarray(s_b), lq1, lk1, lq2, lk2, gsub, layer, batch,
                             seq_len)
        oc = _window_attention(mg, jnp.asarray(s_c), sink[layer], batch, seq_len)
        x, h2, aff = _merge(x, mod, oa_parts, ob, oc, mg, wa, wb, wc, wo, g2, w_router, layer,
                            seq_len)
        aff_t = aff.T

        idx_parts, gate_parts = [], []
        start = 0
        for size in group_sizes:
            cap = (EC_CAPACITY_FACTOR * size) // N_EXPERTS
            gate, idx = lax.top_k(aff_t[:, start:start + size], cap)
            idx_parts.append(idx + start)
            gate_parts.append(gate)
            start += size
        idx = jnp.concatenate(idx_parts, axis=1)
        gate = jnp.concatenate(gate_parts, axis=1)
        idx, gate = lax.sort((idx, gate), dimension=1, num_keys=1)
        xe = jnp.take(h2, idx, axis=0)
        ye = _expert_ffn(xe, gate[..., None], weg, weu, wed, layer)
        if layer == depth - 1:
            x = _combine(x, ye, idx, mod, g_final.reshape(1, d), layer, seq_len, True)
        else:
            x, h1 = _combine(x, ye, idx, mod, g1, layer, seq_len, False)
    return x


def kernel(x_prompt, x_sample, c_prompt, c_sample, g_norm1, g_norm2, w_ada, b_ada, w_in, w_branch_gate, w_br_a, w_br_b, w_br_c, w_o, lambda_q1, lambda_k1, lambda_q2, lambda_k2, g_subln, sink, w_router, w_e_gate, w_e_up, w_e_down, g_final):
    bp, seq_len, d = x_prompt.shape
    bs = x_sample.shape[0]
    assert x_sample.shape[1] == seq_len
    x = jnp.concatenate([x_prompt.reshape(bp * seq_len, d), x_sample.reshape(bs * seq_len, d)], axis=0)
    c = jnp.concatenate([c_prompt, c_sample], axis=0)
    pad = -c.shape[0] % SUBLANES
    c_rows = jnp.pad(c, ((0, pad), (0, 0)))
    y = _trunk(x, c_rows, (bp * seq_len, bs * seq_len), seq_len, g_norm1, g_norm2, w_ada, b_ada,
               w_in, w_branch_gate, w_br_a, w_br_b, w_br_c, w_o, lambda_q1, lambda_k1, lambda_q2,
               lambda_k2, g_subln, sink, w_router, w_e_gate, w_e_up, w_e_down, g_final)
    y_prompt = y[:bp * seq_len].reshape(bp, seq_len, d)
    y_sample = y[bp * seq_len:].reshape(bs, seq_len, d)
    return (y_prompt, y_sample)
```

```python
import functools
import math

import numpy as np
import jax
import jax.numpy as jnp
from jax import lax
from jax.experimental import pallas as pl
from jax.experimental.pallas import tpu as pltpu

F32 = jnp.float32
BF16 = jnp.bfloat16

A_GROUPS = ((128, 1), (512, 4), (2048, 16))
A_HEADS_PER_GROUP = 4
A_HEAD_DIM = 64
A_N_HEADS = 12
A_OUT = 256
A_GROUP_COLS = A_HEADS_PER_GROUP * A_HEAD_DIM
A_TILE = 3 * A_GROUP_COLS
B_HEADS = 6
B_QK_DIM = 64
B_V_DIM = 128
B_OUT = 768
C_Q_HEADS = 8
C_KV_HEADS = 2
C_GROUP = C_Q_HEADS // C_KV_HEADS
C_HEAD_DIM = 128
C_HALF_WINDOW = 128
C_OUT = 1024
A_COLS = 2304
B_COLS = 2304
C_COLS = 1536
IN_COLS = 6144
N_BRANCH = 3
N_EXPERTS = 16
EC_CAPACITY_FACTOR = 2
N_ALIBI_HEADS = 26
RMS_EPS = 1e-6
NEG_INF = -1e30
LOG2E = math.log2(math.e)
N_MOD = 6
SUBLANES = 8
LANES = 128
POS_SPLIT = 64
N_PIECES = 3

_NT_DIMS = (((1,), (1,)), ((), ()))
_MIB = 1024 * 1024


def _alibi_slopes():
    n = N_ALIBI_HEADS
    s = 2.0 ** (-8.0 * np.arange(1, n + 1, dtype=np.float32) / n)
    s = s.astype(np.float32)
    s_c = s[:C_Q_HEADS]
    s_a = s[C_Q_HEADS:C_Q_HEADS + A_N_HEADS]
    s_b = s[C_Q_HEADS + A_N_HEADS:]
    return s_a, s_b, s_c


def _params(semantics, vmem_mib):
    return pltpu.CompilerParams(dimension_semantics=semantics,
                                vmem_limit_bytes=vmem_mib * _MIB)


def _mod_spec(layer, chunk, seq_of):
    def index(*ids):
        return (layer, seq_of(*ids), chunk, 0, 0)
    return index


def _rms_mod(x, g, scale, shift):
    ms = jnp.mean(x * x, axis=-1, keepdims=True)
    y = x * lax.rsqrt(ms + RMS_EPS) * g
    return y * (1.0 + scale) + shift


def _split3(x):
    hi = x.astype(BF16).astype(F32)
    mid = (x - hi).astype(BF16).astype(F32)
    lo = (x - hi - mid).astype(BF16).astype(F32)
    return hi, mid, lo


def _ada_kernel(c_ref, w_ref, b_ref, o_ref):
    c = c_ref[...]
    a = c * jax.nn.sigmoid(c)
    o_ref[...] = jnp.dot(a, w_ref[...], precision=lax.Precision.HIGHEST,
                         preferred_element_type=F32) + b_ref[...]


def _ada_modulation(c_rows, w_ada, b_ada, tn=1024):
    depth, d, n6 = w_ada.shape
    rows = c_rows.shape[0]
    return pl.pallas_call(
        _ada_kernel,
        out_shape=jax.ShapeDtypeStruct((depth, rows, n6), F32),
        grid=(depth, n6 // tn),
        in_specs=[
            pl.BlockSpec((rows, d), lambda l, j: (0, 0)),
            pl.BlockSpec((None, d, tn), lambda l, j: (l, 0, j)),
            pl.BlockSpec((None, 1, tn), lambda l, j: (l, 0, j)),
        ],
        out_specs=pl.BlockSpec((None, rows, tn), lambda l, j: (l, 0, j)),
        compiler_params=_params(("arbitrary", "arbitrary"), 40),
        name="ada_modulation",
    )(c_rows, w_ada, b_ada.reshape(depth, 1, n6))


def _two_part_specs(tm, d, split):
    first = pl.BlockSpec((tm, d), lambda i, *_: (jnp.minimum(i, split - 1), 0))
    second = pl.BlockSpec((tm, d), lambda i, *_: (jnp.maximum(i - split, 0), 0))
    return first, second


def _two_part_load(xa_ref, xb_ref, split):
    return jnp.where(pl.program_id(0) < split, xa_ref[...], xb_ref[...])


def _norm_kernel(xa_ref, xb_ref, g_ref, sc_ref, sh_ref, h_ref, *, split):
    x = _two_part_load(xa_ref, xb_ref, split)
    h_ref[...] = _rms_mod(x, g_ref[...], sc_ref[...], sh_ref[...]).astype(BF16)


def _norm_mod(xa, xb, g, mod, layer, seq_len, tm=512):
    d = xa.shape[1]
    n = xa.shape[0] + xb.shape[0]
    tm = min(tm, seq_len)
    tps = seq_len // tm
    seq_of = lambda i: i // tps
    split = xa.shape[0] // tm
    return pl.pallas_call(
        functools.partial(_norm_kernel, split=split),
        out_shape=jax.ShapeDtypeStruct((n, d), BF16),
        grid=(n // tm,),
        in_specs=[*_two_part_specs(tm, d, split),
                  pl.BlockSpec((None, 1, d), lambda i: (layer, 0, 0)),
                  pl.BlockSpec((None, None, None, 1, d), _mod_spec(layer, 1, seq_of)),
                  pl.BlockSpec((None, None, None, 1, d), _mod_spec(layer, 0, seq_of))],
        out_specs=pl.BlockSpec((tm, d), lambda i: (i, 0)),
        compiler_params=_params(("arbitrary",), 32),
        name="norm_mod",
    )(xa, xb, g, mod, mod)


PROJ_TILE = A_TILE
N_A_TILES = len(A_GROUPS)
N_GATE_TILES = IN_COLS // PROJ_TILE
N_MIX_TILES = (C_COLS + B_COLS) // PROJ_TILE
PROJ_ROW_SPLIT = 2
MG_C_Q = IN_COLS
MG_C_K = MG_C_Q + C_OUT
MG_C_V = MG_C_K + C_KV_HEADS * C_HEAD_DIM
MG_B_Q = MG_C_Q + C_COLS
MG_B_K = MG_B_Q + B_OUT
MG_B_V = MG_B_K + B_OUT


def _proj_kernel(h_ref, w_ref, cs_ref, a0_ref, a1_ref, a2_ref, mg_ref, acc_scr, *, tm):
    j = pl.program_id(1)
    nslab = PROJ_TILE // LANES

    @pl.when(j < N_A_TILES)
    def _():
        acc = jnp.dot(h_ref[...], w_ref[...], preferred_element_type=F32) * cs_ref[...]

        @pl.when(j == 0)
        def _():
            a0_ref[0] = acc.astype(BF16)

        @pl.when(j > 0)
        def _():
            for c in range(nslab):
                acc_scr[c] = acc[:, c * LANES:(c + 1) * LANES]

        for gi, a_ref in ((1, a1_ref), (2, a2_ref)):
            dil = A_GROUPS[gi][1]

            @pl.when(j == gi)
            def _(a_ref=a_ref, dil=dil):
                for r in range(dil):
                    rows = [acc_scr[c, pl.ds(r, tm // dil, stride=dil), :] for c in range(nslab)]
                    a_ref[r] = jnp.concatenate(rows, axis=1).astype(BF16)

    @pl.when(j >= N_A_TILES)
    def _():
        is_gate = j < N_A_TILES + N_GATE_TILES
        for part in range(PROJ_ROW_SPLIT):
            rows = slice(part * tm // PROJ_ROW_SPLIT, (part + 1) * tm // PROJ_ROW_SPLIT)
            acc = jnp.dot(h_ref[rows, :], w_ref[...], preferred_element_type=F32) * cs_ref[...]
            mg_ref[rows, :] = jnp.where(is_gate, 0.5 * jnp.tanh(0.5 * acc) + 0.5, acc).astype(BF16)


def _proj(h, w_cat, col_scale, layer, batch, seq_len, tm=1024):
    n, d = h.shape
    tn = PROJ_TILE
    tm = min(tm, seq_len)
    tps = seq_len // tm
    a_shapes, a_specs = [], []
    for _, dil in A_GROUPS:
        a_shapes.append(jax.ShapeDtypeStruct((batch, dil, seq_len // dil, tn), BF16))
        a_specs.append(pl.BlockSpec((None, dil, tm // dil, tn), lambda i, j: (i // tps, 0, i % tps, 0)))
    n_mg = N_GATE_TILES + N_MIX_TILES
    kern = functools.partial(_proj_kernel, tm=tm)
    return pl.pallas_call(
        kern,
        out_shape=(*a_shapes, jax.ShapeDtypeStruct((n, n_mg * tn), BF16)),
        grid=(n // tm, N_A_TILES + n_mg),
        in_specs=[
            pl.BlockSpec((tm, d), lambda i, j: (i, 0)),
            pl.BlockSpec((None, d, tn), lambda i, j: (layer, 0, j)),
            pl.BlockSpec((1, tn), lambda i, j: (0, j)),
        ],
        out_specs=(*a_specs,
                   pl.BlockSpec((tm, tn), lambda i, j: (i, jnp.clip(j - N_A_TILES, 0, n_mg - 1)))),
        scratch_shapes=[pltpu.VMEM((tn // LANES, tm, LANES), F32)],
        compiler_params=_params(("arbitrary", "arbitrary"), 48),
        name="proj",
    )(h, w_cat, col_scale)


def _proj_weights(w_in, w_branch_gate):
    cols = []
    for gi in range(len(A_GROUPS)):
        for part in range(3):
            start = part * (A_COLS // 3) + gi * A_GROUP_COLS
            cols.append(np.arange(start, start + A_GROUP_COLS))
    perm_a = np.concatenate(cols)
    c0 = A_COLS + B_COLS
    w_cat = jnp.concatenate([w_in[..., perm_a], w_branch_gate, w_in[..., c0:c0 + C_COLS],
                             w_in[..., A_COLS:A_COLS + B_COLS]], axis=-1).astype(BF16)

    scale = np.ones((2 * IN_COLS,), np.float32)
    for gi in range(len(A_GROUPS)):
        scale[gi * A_TILE:gi * A_TILE + A_GROUP_COLS] = A_HEAD_DIM ** -0.5 * LOG2E
    qc = A_COLS + MG_C_Q
    scale[qc:qc + C_OUT] = C_HEAD_DIM ** -0.5 * LOG2E
    qb = A_COLS + MG_B_Q
    scale[qb:qb + B_OUT] = B_QK_DIM ** -0.5 * LOG2E
    return w_cat, jnp.asarray(scale).reshape(1, -1)


def _position_features(seq_len):
    j = np.arange(seq_len)
    feat = np.zeros((seq_len, LANES), np.float32)
    feat[:, 0:N_PIECES] = ((j // POS_SPLIT) * POS_SPLIT)[:, None]
    feat[:, N_PIECES:2 * N_PIECES] = (j % POS_SPLIT)[:, None]
    feat[:, 2 * N_PIECES:3 * N_PIECES] = 1.0
    return jnp.asarray(feat, BF16)


ONES_COL = B_V_DIM + 2 * N_PIECES


def _battn_kernel(slope_ref, q_ref, k_ref, v_ref, feat_ref, lq1_ref, lk1_ref, lq2_ref, lk2_ref,
                  gs_ref, o_ref, *, tq, tk, seq_len, lam_init):
    h = pl.program_id(1)
    qi = pl.program_id(2)
    slope = slope_ref[h] * LOG2E
    q = q_ref[...]
    lane = lax.broadcasted_iota(jnp.int32, (tq, LANES), 1)
    zero = jnp.zeros_like(q)
    qq = jnp.concatenate([jnp.where(lane < B_QK_DIM, q, zero),
                          jnp.where(lane >= B_QK_DIM, q, zero)], axis=0)
    row = lax.broadcasted_iota(jnp.int32, (2 * tq, 1), 0)
    qpos = qi * tq + jnp.where(row >= tq, row - tq, row)

    lane2 = lax.broadcasted_iota(jnp.int32, (2 * tq, LANES), 1)
    piece = lane2 % N_PIECES
    s_hi, s_mid, s_lo = _split3(jnp.full((1, LANES), slope, F32))
    c_hi, c_mid, c_lo = _split3(slope * qpos.astype(F32))
    s_piece = jnp.where(piece == 0, s_hi, jnp.where(piece == 1, s_mid, s_lo))
    c_piece = jnp.where(piece == 0, c_hi, jnp.where(piece == 1, c_mid, c_lo))
    aug = jnp.where(lane2 < 2 * N_PIECES, -s_piece, jnp.where(lane2 < 3 * N_PIECES, c_piece, 0.0))
    lhs_right = jnp.concatenate([qq, aug.astype(BF16)], axis=1)
    lhs_left = jnp.concatenate([qq, (-aug).astype(BF16)], axis=1)

    nchunks = seq_len // tk
    jd = (qi * tq) // tk

    def chunk(t, carry, diag):
        m, acc = carry
        j = lax.rem(jd + t, nchunks)
        k0 = pl.multiple_of(j * tk, tk)
        kc = k_ref[pl.ds(k0, tk), :]
        fc = feat_ref[pl.ds(k0, tk), :]
        vf = jnp.concatenate([v_ref[pl.ds(k0, tk), :], fc], axis=1)
        if diag:
            s = lax.dot_general(qq, kc, _NT_DIMS, preferred_element_type=F32)
            kpos = k0 + lax.broadcasted_iota(jnp.int32, (1, tk), 1)
            s = s - slope * jnp.abs(qpos - kpos).astype(F32)
        else:
            lhs = jnp.where(j < jd, lhs_left, lhs_right)
            s = lax.dot_general(lhs, jnp.concatenate([kc, fc], axis=1), _NT_DIMS,
                                preferred_element_type=F32)
        m_new = jnp.maximum(m, jnp.max(s, axis=-1, keepdims=True))
        alpha = jnp.exp2(m - m_new)
        p = jnp.exp2(s - m_new).astype(BF16)
        acc = alpha * acc + jnp.dot(p, vf, preferred_element_type=F32)
        return m_new, acc

    carry = (jnp.full((2 * tq, 1), NEG_INF, F32), jnp.zeros((2 * tq, 2 * LANES), F32))
    carry = chunk(0, carry, True)
    for t in range(1, nchunks):
        carry = chunk(t, carry, False)
    _, acc = carry

    o = acc[:, :B_V_DIM] / acc[:, ONES_COL:ONES_COL + 1]
    lam = (jnp.exp(jnp.sum(lq1_ref[...] * lk1_ref[...], axis=-1, keepdims=True))
           - jnp.exp(jnp.sum(lq2_ref[...] * lk2_ref[...], axis=-1, keepdims=True)) + lam_init)
    diff = o[:tq] - lam * o[tq:]
    ms = jnp.mean(diff * diff, axis=-1, keepdims=True)
    y = diff * lax.rsqrt(ms + RMS_EPS) * gs_ref[...]
    o_ref[...] = (y * (1.0 - lam_init)).astype(BF16)


def _diff_attention(mix, feat, slopes_b, lq1, lk1, lq2, lk2, g_subln, layer, batch, seq_len,
                    tq=512, tk=512):
    n = mix.shape[0]
    tq = min(tq, seq_len)
    tk = min(tk, seq_len)
    nq = seq_len // tq
    qcol, kcol, vcol = (c // B_V_DIM for c in (MG_B_Q, MG_B_K, MG_B_V))
    lam_init = 0.8 - 0.6 * math.exp(-0.3 * layer)
    kern = functools.partial(_battn_kernel, tq=tq, tk=tk, seq_len=seq_len, lam_init=lam_init)
    vec = lambda width: pl.BlockSpec((None, 1, width), lambda b, h, i: (layer, 0, 0))
    return pl.pallas_call(
        kern,
        out_shape=jax.ShapeDtypeStruct((n, B_OUT), BF16),
        grid=(batch, B_HEADS, nq),
        in_specs=[
            pl.BlockSpec(memory_space=pltpu.SMEM),
            pl.BlockSpec((tq, B_V_DIM), lambda b, h, i: (b * nq + i, qcol + h)),
            pl.BlockSpec((seq_len, B_V_DIM), lambda b, h, i: (b, kcol + h)),
            pl.BlockSpec((seq_len, B_V_DIM), lambda b, h, i: (b, vcol + h)),
            pl.BlockSpec((seq_len, LANES), lambda b, h, i: (0, 0)),
            vec(B_QK_DIM), vec(B_QK_DIM), vec(B_QK_DIM), vec(B_QK_DIM), vec(B_V_DIM),
        ],
        out_specs=pl.BlockSpec((tq, B_V_DIM), lambda b, h, i: (b * nq + i, h)),
        compiler_params=_params(("arbitrary", "arbitrary", "arbitrary"), 48),
        name="diff_attention",
    )(slopes_b, mix, mix, mix, feat, lq1, lk1, lq2, lk2, g_subln)


def _band_window(qi, tq, half, length):
    kw = tq + 2 * half
    ks = jnp.clip(qi * tq - half, 0, length - kw)
    ks = pl.multiple_of(ks, half)
    qpos = qi * tq + lax.broadcasted_iota(jnp.int32, (tq, 1), 0)
    kpos = ks + lax.broadcasted_iota(jnp.int32, (1, kw), 1)
    rel = jnp.abs(qpos - kpos)
    return ks, kw, rel <= half, rel.astype(F32)


def _cattn_kernel(slope_ref, sink_ref, q_ref, k_ref, v_ref, o_ref, *, tq, seq_len):
    g = pl.program_id(1)
    qi = pl.program_id(2)
    ks, kw, valid, relf = _band_window(qi, tq, C_HALF_WINDOW, seq_len)
    kwin = k_ref[pl.ds(ks, kw), :]
    vwin = v_ref[pl.ds(ks, kw), :]
    for hh in range(C_GROUP):
        head = g * C_GROUP + hh
        slope = slope_ref[head] * LOG2E
        sink = sink_ref[head] * LOG2E
        cols = slice(hh * C_HEAD_DIM, (hh + 1) * C_HEAD_DIM)
        s = lax.dot_general(q_ref[:, cols], kwin, _NT_DIMS, preferred_element_type=F32)
        s = jnp.where(valid, s - slope * relf, NEG_INF)
        m = jnp.maximum(jnp.max(s, axis=-1, keepdims=True), sink)
        p = jnp.exp2(s - m)
        den = jnp.sum(p, axis=-1, keepdims=True) + jnp.exp2(sink - m)
        o = jnp.dot(p.astype(BF16), vwin, preferred_element_type=F32) / den
        o_ref[:, cols] = o.astype(BF16)


def _window_attention(mix, slopes_c, sink, batch, seq_len, tq=256):
    n = mix.shape[0]
    nq = seq_len // tq
    gw = C_GROUP * C_HEAD_DIM
    qcol = MG_C_Q // gw
    kcol = MG_C_K // C_HEAD_DIM
    vcol = MG_C_V // C_HEAD_DIM
    kern = functools.partial(_cattn_kernel, tq=tq, seq_len=seq_len)
    return pl.pallas_call(
        kern,
        out_shape=jax.ShapeDtypeStruct((n, C_OUT), BF16),
        grid=(batch, C_KV_HEADS, nq),
        in_specs=[
            pl.BlockSpec(memory_space=pltpu.SMEM),
            pl.BlockSpec(memory_space=pltpu.SMEM),
            pl.BlockSpec((tq, gw), lambda b, g, i: (b * nq + i, qcol + g)),
            pl.BlockSpec((seq_len, C_HEAD_DIM), lambda b, g, i: (b, kcol + g)),
            pl.BlockSpec((seq_len, C_HEAD_DIM), lambda b, g, i: (b, vcol + g)),
        ],
        out_specs=pl.BlockSpec((tq, gw), lambda b, g, i: (b * nq + i, g)),
        compiler_params=_params(("arbitrary", "arbitrary", "arbitrary"), 32),
        name="window_attention",
    )(slopes_c, sink, mix, mix, mix)


def _aattn_kernel(q_ref, k_ref, v_ref, o_ref, lse_ref, *scratch, tq, length, slopes, dil):
    qi = pl.program_id(1)
    nslab = A_GROUP_COLS // LANES
    half = A_GROUPS[0][0] // 2
    ks, kw, valid, relf = _band_window(qi, tq, half, length)
    lane = lax.broadcasted_iota(jnp.int32, (tq, A_GROUP_COLS), 1)

    def residue(r, _):
        kwin = k_ref[r, pl.ds(ks, kw), :]
        vwin = v_ref[r, pl.ds(ks, kw), :]
        q = q_ref[r]
        zero = jnp.zeros_like(q)
        out = jnp.zeros((tq, A_GROUP_COLS), F32)
        lse_out = jnp.zeros((tq, A_GROUP_COLS), F32)
        for hh in range(A_HEADS_PER_GROUP):
            in_head = (lane >= hh * A_HEAD_DIM) & (lane < (hh + 1) * A_HEAD_DIM)
            s = lax.dot_general(jnp.where(in_head, q, zero), kwin, _NT_DIMS,
                                preferred_element_type=F32)
            s = jnp.where(valid, s - (slopes[hh] * dil * LOG2E) * relf, NEG_INF)
            m = jnp.max(s, axis=-1, keepdims=True)
            p = jnp.exp2(s - m)
            den = jnp.sum(p, axis=-1, keepdims=True)
            o = jnp.dot(p.astype(BF16), vwin, preferred_element_type=F32) / den
            out = jnp.where(in_head, o, out)
            lse_out = jnp.where(in_head, m + jnp.log2(den), lse_out)
        if dil == 1:
            o_ref[...] = out
            lse_ref[...] = lse_out
        else:
            o_scr, lse_scr = scratch
            for c in range(nslab):
                cols = slice(c * LANES, (c + 1) * LANES)
                o_scr[c, pl.ds(r, tq, stride=dil), :] = out[:, cols]
                lse_scr[c, pl.ds(r, tq, stride=dil), :] = lse_out[:, cols]
        return 0

    if dil == 1:
        residue(0, 0)
    else:
        lax.fori_loop(0, dil, residue, 0)
        o_scr, lse_scr = scratch
        o_ref[...] = jnp.concatenate([o_scr[c] for c in range(nslab)], axis=1)
        lse_ref[...] = jnp.concatenate([lse_scr[c] for c in range(nslab)], axis=1)


def _dilated_attention(qkv, group, slopes, batch, seq_len, tq=256):
    window, dil = A_GROUPS[group]
    half = window // (2 * dil)
    length = seq_len // dil
    tq = min(tq, length - 2 * half)
    nq = length // tq
    n = batch * seq_len
    kern = functools.partial(_aattn_kernel, tq=tq, length=length,
                             slopes=tuple(float(v) for v in slopes), dil=dil)
    out_sds = jax.ShapeDtypeStruct((n, A_GROUP_COLS), F32)
    out_spec = pl.BlockSpec((tq * dil, A_GROUP_COLS), lambda b, i: (b * nq + i, 0))
    kv_spec = lambda part: pl.BlockSpec((None, dil, length, A_GROUP_COLS), lambda b, i: (b, 0, 0, part),
                                        pipeline_mode=pl.Buffered(1))
    return pl.pallas_call(
        kern,
        out_shape=(out_sds, out_sds),
        grid=(batch, nq),
        in_specs=[
            pl.BlockSpec((None, dil, tq, A_GROUP_COLS), lambda b, i: (b, 0, i, 0)),
            kv_spec(1), kv_spec(2),
        ],
        out_specs=(out_spec, out_spec),
        scratch_shapes=[] if dil == 1 else
        [pltpu.VMEM((A_GROUP_COLS // LANES, tq * dil, LANES), F32)] * 2,
        compiler_params=_params(("arbitrary", "arbitrary"), 48),
        name=f"dilated_attention_g{group}",
    )(qkv, qkv, qkv)


def _merge_kernel(xa_ref, xb_ref, gt_ref, o0_ref, o1_ref, o2_ref, l0_ref, l1_ref, l2_ref, ob_ref, oc_ref,
                  ga_ref, gb_ref, gc_ref, wa_ref, wb_ref, wc_ref, wo_ref, g2_ref, sc2_ref, sh2_ref,
                  wr_ref, out_ref, h_ref, aff_ref, *, split):
    l0, l1, l2 = l0_ref[...], l1_ref[...], l2_ref[...]
    m = jnp.maximum(jnp.maximum(l0, l1), l2)
    e0, e1, e2 = jnp.exp2(l0 - m), jnp.exp2(l1 - m), jnp.exp2(l2 - m)
    oa = (e0 * o0_ref[...] + e1 * o1_ref[...] + e2 * o2_ref[...]) / (e0 + e1 + e2)
    ya = jnp.dot(oa.astype(BF16), wa_ref[...], preferred_element_type=F32)
    yb = jnp.dot(ob_ref[...], wb_ref[...], preferred_element_type=F32)
    yc = jnp.dot(oc_ref[...], wc_ref[...], preferred_element_type=F32)
    merged = (ga_ref[...].astype(F32) * ya + gb_ref[...].astype(F32) * yb
              + gc_ref[...].astype(F32) * yc)
    y = jnp.dot(merged.astype(BF16), wo_ref[...], preferred_element_type=F32)
    x = _two_part_load(xa_ref, xb_ref, split) + gt_ref[...] * y
    out_ref[...] = x

    h = _rms_mod(x, g2_ref[...], sc2_ref[...], sh2_ref[...])
    h_ref[...] = h.astype(BF16)
    w = wr_ref[...]
    h_hi = h.astype(BF16)
    h_lo = (h - h_hi.astype(F32)).astype(BF16)
    w_hi = w.astype(BF16)
    w_lo = (w - w_hi.astype(F32)).astype(BF16)
    logits = (jnp.dot(h_hi, w_hi, preferred_element_type=F32)
              + jnp.dot(h_lo, w_hi, preferred_element_type=F32)
              + jnp.dot(h_hi, w_lo, preferred_element_type=F32))
    mx = jnp.max(logits, axis=-1, keepdims=True)
    e = jnp.exp(logits - mx)
    aff_ref[...] = e / jnp.sum(e, axis=-1, keepdims=True)


def _merge(xa, xb, mod, oa_parts, ob, oc, mg, w_br_a, w_br_b, w_br_c, w_o, g2, w_router, layer,
           seq_len, tm=256):
    d = xa.shape[1]
    n = mg.shape[0]
    tm = min(tm, seq_len)
    split = min(xa.shape[0], n) // tm
    tiles_per_seq = seq_len // tm
    seq_of = lambda i: i // tiles_per_seq
    ne = w_router.shape[-1]
    (o0, l0), (o1, l1), (o2, l2) = oa_parts
    row = lambda width: pl.BlockSpec((tm, width), lambda i: (i, 0))
    resident = lambda rows: pl.BlockSpec((None, rows, d), lambda i: (layer, 0, 0),
                                         pipeline_mode=pl.Buffered(1))
    gate = lambda k: pl.BlockSpec((tm, d), lambda i: (i, k))
    modrow = lambda chunk: pl.BlockSpec((None, None, None, 1, d), _mod_spec(layer, chunk, seq_of))
    return pl.pallas_call(
        functools.partial(_merge_kernel, split=split),
        out_shape=(jax.ShapeDtypeStruct((n, d), F32), jax.ShapeDtypeStruct((n, d), BF16),
                   jax.ShapeDtypeStruct((n, ne), F32)),
        grid=(n // tm,),
        in_specs=[
            *_two_part_specs(tm, d, split), modrow(2),
            row(A_OUT), row(A_OUT), row(A_OUT), row(A_OUT), row(A_OUT), row(A_OUT),
            row(B_OUT), row(C_OUT),
            gate(0), gate(1), gate(2),
            resident(A_OUT), resident(B_OUT), resident(C_OUT), resident(d),
            pl.BlockSpec((None, 1, d), lambda i: (layer, 0, 0)), modrow(4), modrow(3),
            pl.BlockSpec((None, d, ne), lambda i: (layer, 0, 0)),
        ],
        out_specs=(row(d), row(d), row(ne)),
        compiler_params=_params(("arbitrary",), 56),
        name="merge",
    )(xa, xb, mod, o0, o1, o2, l0, l1, l2, ob, oc, mg, mg, mg, w_br_a, w_br_b, w_br_c, w_o,
      g2, mod, mod, w_router)


def _ffn_kernel(x_ref, gate_ref, wg_ref, wu_ref, wd_ref, o_ref, acc_scr):
    f = pl.program_id(2)

    @pl.when(f == 0)
    def _():
        acc_scr[...] = jnp.zeros_like(acc_scr)

    x = x_ref[...]
    a = jnp.dot(x, wg_ref[...], preferred_element_type=F32)
    u = jnp.dot(x, wu_ref[...], preferred_element_type=F32)
    hid = (a * jax.nn.sigmoid(a) * u).astype(BF16)
    acc_scr[...] += jnp.dot(hid, wd_ref[...], preferred_element_type=F32)

    @pl.when(f == pl.num_programs(2) - 1)
    def _():
        o_ref[...] = (acc_scr[...] * gate_ref[...]).astype(BF16)


def _expert_ffn(xe, gate, w_gate, w_up, w_down, layer, tm=1024, tf=512):
    ne, cap, d = xe.shape
    dff = w_gate.shape[-1]
    tm = min(tm, cap)
    return pl.pallas_call(
        _ffn_kernel,
        out_shape=jax.ShapeDtypeStruct((ne, cap, d), BF16),
        grid=(ne, cap // tm, dff // tf),
        in_specs=[
            pl.BlockSpec((None, tm, d), lambda e, i, f: (e, i, 0)),
            pl.BlockSpec((None, tm, 1), lambda e, i, f: (e, i, 0)),
            pl.BlockSpec((None, None, d, tf), lambda e, i, f: (layer, e, 0, f)),
            pl.BlockSpec((None, None, d, tf), lambda e, i, f: (layer, e, 0, f)),
            pl.BlockSpec((None, None, tf, d), lambda e, i, f: (layer, e, f, 0)),
        ],
        out_specs=pl.BlockSpec((None, tm, d), lambda e, i, f: (e, i, 0)),
        scratch_shapes=[pltpu.VMEM((tm, d), F32)],
        compiler_params=_params(("arbitrary", "arbitrary", "arbitrary"), 48),
        name="expert_ffn",
    )(xe, gate, w_gate, w_up, w_down)


WINDOW = 256
N_WINDOW_BUFS = 4


def _combine_copy(ye_hbm, buf, sem, expert, window, slot):
    start = pl.multiple_of(window * WINDOW, WINDOW)
    return pltpu.make_async_copy(ye_hbm.at[expert, pl.ds(start, WINDOW), :], buf.at[slot],
                                 sem.at[slot])


def _combine_kernel(cnt_ref, we_ref, ww_ref, idx_ref, x_ref, gt_ref, g_ref, sc_ref, sh_ref, ye_hbm,
                    o_ref, o2_ref, buf, sem, acc_scr, *, tm, final_norm, split):
    tile = pl.program_id(0)
    n = cnt_ref[tile]
    tokens = tile * tm + lax.broadcasted_iota(jnp.int32, (tm, WINDOW), 0)
    acc_scr[...] = jnp.zeros_like(acc_scr)

    def start(p):
        _combine_copy(ye_hbm, buf, sem, we_ref[tile, p], ww_ref[tile, p],
                      lax.rem(p, N_WINDOW_BUFS)).start()

    for ahead in range(N_WINDOW_BUFS - 1):
        @pl.when(ahead < n)
        def _(ahead=ahead):
            start(ahead)

    def body(p, _):
        slot = lax.rem(p, N_WINDOW_BUFS)
        expert = we_ref[tile, p]
        window = ww_ref[tile, p]
        _combine_copy(ye_hbm, buf, sem, expert, window, slot).wait()

        @pl.when(p + N_WINDOW_BUFS - 1 < n)
        def _():
            start(p + N_WINDOW_BUFS - 1)

        ids = idx_ref[pl.ds(expert, 1), pl.ds(pl.multiple_of(window * WINDOW, WINDOW), WINDOW)]
        onehot = jnp.where(ids == tokens, 1.0, 0.0).astype(BF16)
        acc_scr[...] += jnp.dot(onehot, buf[slot], preferred_element_type=F32)
        return 0

    lax.fori_loop(0, n, body, 0)
    x = x_ref[...] + gt_ref[...] * acc_scr[...]
    if final_norm:
        ms = jnp.mean(x * x, axis=-1, keepdims=True)
        y = x * lax.rsqrt(ms + RMS_EPS) * g_ref[...]

        @pl.when(tile < split)
        def _():
            o_ref[...] = y

        @pl.when(tile >= split)
        def _():
            o2_ref[...] = y
    else:
        o_ref[...] = x
        o2_ref[...] = _rms_mod(x, g_ref[...], sc_ref[...], sh_ref[...]).astype(BF16)


def _combine_plan(idx_sorted, n_tokens, tm):
    ne, cap = idx_sorted.shape
    nt = n_tokens // tm
    bounds = jnp.arange(nt + 1, dtype=jnp.int32) * tm
    base = jnp.sum(idx_sorted[:, :, None] < bounds[None, None, :], axis=1, dtype=jnp.int32)
    lo, hi = base[:, :-1], base[:, 1:]
    first = lo // WINDOW
    nwin = jnp.where(hi > lo, (hi - 1) // WINDOW - first + 1, 0)
    cum_incl = jnp.cumsum(nwin, axis=0)
    cum_excl = cum_incl - nwin
    count = cum_incl[-1]
    max_pairs = ne * (tm // WINDOW + 1)
    p = jnp.arange(max_pairs, dtype=jnp.int32)
    expert = jnp.sum(cum_incl.T[:, None, :] <= p[None, :, None], axis=-1, dtype=jnp.int32)
    expert = jnp.minimum(expert, ne - 1)
    k = p[None, :] - jnp.take_along_axis(cum_excl.T, expert, axis=1)
    window = jnp.take_along_axis(first.T, expert, axis=1) + k
    window = jnp.clip(window, 0, cap // WINDOW - 1)
    return count.astype(jnp.int32), expert, window.astype(jnp.int32)


def _combine(x, ye, idx_sorted, mod, g_next, layer, seq_len, final_norm, n_first=0, tm=512):
    n, d = x.shape
    tm = min(tm, seq_len)
    tiles_per_seq = seq_len // tm
    seq_of = lambda i, *_: i // tiles_per_seq
    count, expert, window = _combine_plan(idx_sorted, n, tm)
    split = n_first // tm
    kern = functools.partial(_combine_kernel, tm=tm, final_norm=final_norm, split=split)
    row = pl.BlockSpec((tm, d), lambda i, *_: (i, 0))
    modrow = lambda lyr, chunk: pl.BlockSpec((None, None, None, 1, d), _mod_spec(lyr, chunk, seq_of))
    if final_norm:
        g_spec = pl.BlockSpec((1, d), lambda i, *_: (0, 0))
        norm_specs = [g_spec, modrow(layer, 1), modrow(layer, 0)]
        out_shape = (jax.ShapeDtypeStruct((n_first, d), F32),
                     jax.ShapeDtypeStruct((n - n_first, d), F32))
        out_specs = _two_part_specs(tm, d, split)
    else:
        g_spec = pl.BlockSpec((None, 1, d), lambda i, *_: (layer + 1, 0, 0))
        norm_specs = [g_spec, modrow(layer + 1, 1), modrow(layer + 1, 0)]
        out_shape = (jax.ShapeDtypeStruct((n, d), F32), jax.ShapeDtypeStruct((n, d), BF16))
        out_specs = (row, row)
    grid_spec = pltpu.PrefetchScalarGridSpec(
        num_scalar_prefetch=3,
        grid=(n // tm,),
        in_specs=[
            pl.BlockSpec(idx_sorted.shape, lambda i, *_: (0, 0)),
            row,
            modrow(layer, N_MOD - 1),
            *norm_specs,
            pl.BlockSpec(memory_space=pl.ANY),
        ],
        out_specs=out_specs,
        scratch_shapes=[pltpu.VMEM((N_WINDOW_BUFS, WINDOW, d), BF16),
                        pltpu.SemaphoreType.DMA((N_WINDOW_BUFS,)),
                        pltpu.VMEM((tm, d), F32)],
    )
    return pl.pallas_call(
        kern,
        out_shape=out_shape,
        grid_spec=grid_spec,
        compiler_params=_params(("arbitrary",), 40),
        name="combine",
    )(count, expert, window, idx_sorted, x, mod, g_next, mod, mod, ye)


def _trunk(xa, xb, c_rows, seq_len, g_norm1, g_norm2, w_ada, b_ada, w_in, w_branch_gate,
           w_br_a, w_br_b, w_br_c, w_o, lambda_q1, lambda_k1, lambda_q2, lambda_k2, g_subln, sink,
           w_router, w_e_gate, w_e_up, w_e_down, g_final):
    d = xa.shape[1]
    group_sizes = (xa.shape[0], xb.shape[0])
    n = sum(group_sizes)
    depth = w_in.shape[0]
    batch = n // seq_len
    s_a, s_b, s_c = _alibi_slopes()

    w_cat, col_scale = _proj_weights(w_in, w_branch_gate)
    wa, wb, wc, wo = (w.astype(BF16) for w in (w_br_a, w_br_b, w_br_c, w_o))
    weg, weu, wed = (w.astype(BF16) for w in (w_e_gate, w_e_up, w_e_down))
    row3 = lambda a: a.reshape(depth, 1, a.shape[-1])
    g1, g2 = row3(g_norm1), row3(g_norm2)
    lq1, lk1, lq2, lk2, gsub = (row3(a) for a in (lambda_q1, lambda_k1, lambda_q2, lambda_k2, g_subln))
    feat = _position_features(seq_len)

    mod = _ada_modulation(c_rows, w_ada, b_ada)
    mod = mod.reshape(depth, c_rows.shape[0], N_MOD, 1, d)

    h1 = _norm_mod(xa, xb, g1, mod, 0, seq_len)
    for layer in range(depth):
        *a_qkv, mg = _proj(h1, w_cat, col_scale, layer, batch, seq_len)
        oa_parts = [
            _dilated_attention(a_qkv[gi], gi,
                               s_a[gi * A_HEADS_PER_GROUP:(gi + 1) * A_HEADS_PER_GROUP],
                               batch, seq_len)
            for gi in range(len(A_GROUPS))
        ]
        ob = _diff_attention(mg, feat, jnp.asarray(s_b), lq1, lk1, lq2, lk2, gsub, layer, batch,
                             seq_len)
        oc = _window_attention(mg, jnp.asarray(s_c), sink[layer], batch, seq_len)
        x, h2, aff = _merge(xa, xb, mod, oa_parts, ob, oc, mg, wa, wb, wc, wo, g2, w_router, layer,
                            seq_len)
        aff_t = aff.T

        idx_parts, gate_parts = [], []
        start = 0
        for size in group_sizes:
            cap = (EC_CAPACITY_FACTOR * size) // N_EXPERTS
            gate, idx = lax.top_k(aff_t[:, start:start + size], cap)
            idx_parts.append(idx + start)
            gate_parts.append(gate)
            start += size
        idx = jnp.concatenate(idx_parts, axis=1)
        gate = jnp.concatenate(gate_parts, axis=1)
        idx, gate = lax.sort((idx, gate), dimension=1, num_keys=1)
        xe = jnp.take(h2, idx, axis=0, mode="clip")
        ye = _expert_ffn(xe, gate[..., None], weg, weu, wed, layer)
        if layer == depth - 1:
            return _combine(x, ye, idx, mod, g_final.reshape(1, d), layer, seq_len, True,
                            n_first=group_sizes[0])
        xa, h1 = _combine(x, ye, idx, mod, g1, layer, seq_len, False)
        xb = xa


def kernel(x_prompt, x_sample, c_prompt, c_sample, g_norm1, g_norm2, w_ada, b_ada, w_in, w_branch_gate, w_br_a, w_br_b, w_br_c, w_o, lambda_q1, lambda_k1, lambda_q2, lambda_k2, g_subln, sink, w_router, w_e_gate, w_e_up, w_e_down, g_final):
    bp, seq_len, d = x_prompt.shape
    bs = x_sample.shape[0]
    assert x_sample.shape[1] == seq_len
    c = jnp.concatenate([c_prompt, c_sample], axis=0)
    pad = -c.shape[0] % SUBLANES
    c_rows = jnp.pad(c, ((0, pad), (0, 0)))
    y_prompt, y_sample = _trunk(
        x_prompt.reshape(bp * seq_len, d), x_sample.reshape(bs * seq_len, d), c_rows, seq_len,
        g_norm1, g_norm2, w_ada, b_ada, w_in, w_branch_gate, w_br_a, w_br_b, w_br_c, w_o,
        lambda_q1, lambda_k1, lambda_q2, lambda_k2, g_subln, sink, w_router, w_e_gate, w_e_up,
        w_e_down, g_final)
    return (y_prompt.reshape(bp, seq_len, d), y_sample.reshape(bs, seq_len, d))
```

```python
import functools
import math

import numpy as np
import jax
import jax.numpy as jnp
from jax import lax
from jax.experimental import pallas as pl
from jax.experimental.pallas import tpu as pltpu

F32 = jnp.float32
BF16 = jnp.bfloat16

A_GROUPS = ((128, 1), (512, 4), (2048, 16))
A_HEADS_PER_GROUP = 4
A_HEAD_DIM = 64
A_N_HEADS = 12
A_OUT = 256
A_GROUP_COLS = A_HEADS_PER_GROUP * A_HEAD_DIM
A_TILE = 3 * A_GROUP_COLS
B_HEADS = 6
B_QK_DIM = 64
B_V_DIM = 128
B_OUT = 768
C_Q_HEADS = 8
C_KV_HEADS = 2
C_GROUP = C_Q_HEADS // C_KV_HEADS
C_HEAD_DIM = 128
C_HALF_WINDOW = 128
C_OUT = 1024
A_COLS = 2304
B_COLS = 2304
C_COLS = 1536
IN_COLS = 6144
N_BRANCH = 3
N_EXPERTS = 16
EC_CAPACITY_FACTOR = 2
N_ALIBI_HEADS = 26
RMS_EPS = 1e-6
NEG_INF = -1e30
LOG2E = math.log2(math.e)
N_MOD = 6
SUBLANES = 8
LANES = 128
POS_SPLIT = 64
N_PIECES = 3

_NT_DIMS = (((1,), (1,)), ((), ()))
_MIB = 1024 * 1024


def _alibi_slopes():
    n = N_ALIBI_HEADS
    s = 2.0 ** (-8.0 * np.arange(1, n + 1, dtype=np.float32) / n)
    s = s.astype(np.float32)
    s_c = s[:C_Q_HEADS]
    s_a = s[C_Q_HEADS:C_Q_HEADS + A_N_HEADS]
    s_b = s[C_Q_HEADS + A_N_HEADS:]
    return s_a, s_b, s_c


def _params(semantics, vmem_mib):
    return pltpu.CompilerParams(dimension_semantics=semantics,
                                vmem_limit_bytes=vmem_mib * _MIB)


def _mod_spec(layer, chunk, seq_of):
    def index(*ids):
        return (layer, seq_of(*ids), chunk, 0, 0)
    return index


def _rms_mod(x, g, scale, shift):
    ms = jnp.mean(x * x, axis=-1, keepdims=True)
    y = x * lax.rsqrt(ms + RMS_EPS) * g
    return y * (1.0 + scale) + shift


def _split3(x):
    hi = x.astype(BF16).astype(F32)
    mid = (x - hi).astype(BF16).astype(F32)
    lo = (x - hi - mid).astype(BF16).astype(F32)
    return hi, mid, lo


def _ada_kernel(c_ref, w_ref, b_ref, o_ref):
    c = c_ref[...]
    a = c * jax.nn.sigmoid(c)
    o_ref[...] = jnp.dot(a, w_ref[...], precision=lax.Precision.HIGHEST,
                         preferred_element_type=F32) + b_ref[...]


def _ada_modulation(c_rows, w_ada, b_ada, tn=1024):
    depth, d, n6 = w_ada.shape
    rows = c_rows.shape[0]
    return pl.pallas_call(
        _ada_kernel,
        out_shape=jax.ShapeDtypeStruct((depth, rows, n6), F32),
        grid=(depth, n6 // tn),
        in_specs=[
            pl.BlockSpec((rows, d), lambda l, j: (0, 0)),
            pl.BlockSpec((None, d, tn), lambda l, j: (l, 0, j)),
            pl.BlockSpec((None, 1, tn), lambda l, j: (l, 0, j)),
        ],
        out_specs=pl.BlockSpec((None, rows, tn), lambda l, j: (l, 0, j)),
        compiler_params=_params(("arbitrary", "arbitrary"), 40),
        name="ada_modulation",
    )(c_rows, w_ada, b_ada.reshape(depth, 1, n6))


def _two_part_specs(tm, d, split):
    first = pl.BlockSpec((tm, d), lambda i, *_: (jnp.minimum(i, split - 1), 0))
    second = pl.BlockSpec((tm, d), lambda i, *_: (jnp.maximum(i - split, 0), 0))
    return first, second


def _two_part_load(xa_ref, xb_ref, split):
    return jnp.where(pl.program_id(0) < split, xa_ref[...], xb_ref[...])


def _norm_kernel(xa_ref, xb_ref, g_ref, sc_ref, sh_ref, h_ref, *, split):
    x = _two_part_load(xa_ref, xb_ref, split)
    h_ref[...] = _rms_mod(x, g_ref[...], sc_ref[...], sh_ref[...]).astype(BF16)


def _norm_mod(xa, xb, g, mod, layer, seq_len, tm=512):
    d = xa.shape[1]
    n = xa.shape[0] + xb.shape[0]
    tm = min(tm, seq_len)
    tps = seq_len // tm
    seq_of = lambda i: i // tps
    split = xa.shape[0] // tm
    return pl.pallas_call(
        functools.partial(_norm_kernel, split=split),
        out_shape=jax.ShapeDtypeStruct((n, d), BF16),
        grid=(n // tm,),
        in_specs=[*_two_part_specs(tm, d, split),
                  pl.BlockSpec((None, 1, d), lambda i: (layer, 0, 0)),
                  pl.BlockSpec((None, None, None, 1, d), _mod_spec(layer, 1, seq_of)),
                  pl.BlockSpec((None, None, None, 1, d), _mod_spec(layer, 0, seq_of))],
        out_specs=pl.BlockSpec((tm, d), lambda i: (i, 0)),
        compiler_params=_params(("arbitrary",), 32),
        name="norm_mod",
    )(xa, xb, g, mod, mod)


PROJ_TILE = A_TILE
N_A_TILES = len(A_GROUPS)
N_GATE_TILES = IN_COLS // PROJ_TILE
N_MIX_TILES = (C_COLS + B_COLS) // PROJ_TILE
PROJ_ROW_SPLIT = 2
MG_C_Q = IN_COLS
MG_C_K = MG_C_Q + C_OUT
MG_C_V = MG_C_K + C_KV_HEADS * C_HEAD_DIM
MG_B_Q = MG_C_Q + C_COLS
MG_B_K = MG_B_Q + B_OUT
MG_B_V = MG_B_K + B_OUT


def _proj_kernel(h_ref, w_ref, cs_ref, a0_ref, a1_ref, a2_ref, mg_ref, acc_scr, *, tm):
    j = pl.program_id(1)
    nslab = PROJ_TILE // LANES

    @pl.when(j < N_A_TILES)
    def _():
        acc = jnp.dot(h_ref[...], w_ref[...], preferred_element_type=F32) * cs_ref[...]

        @pl.when(j == 0)
        def _():
            a0_ref[0] = acc.astype(BF16)

        @pl.when(j > 0)
        def _():
            for c in range(nslab):
                acc_scr[c] = acc[:, c * LANES:(c + 1) * LANES]

        for gi, a_ref in ((1, a1_ref), (2, a2_ref)):
            dil = A_GROUPS[gi][1]

            @pl.when(j == gi)
            def _(a_ref=a_ref, dil=dil):
                for r in range(dil):
                    rows = [acc_scr[c, pl.ds(r, tm // dil, stride=dil), :] for c in range(nslab)]
                    a_ref[r] = jnp.concatenate(rows, axis=1).astype(BF16)

    @pl.when(j >= N_A_TILES)
    def _():
        is_gate = j < N_A_TILES + N_GATE_TILES
        for part in range(PROJ_ROW_SPLIT):
            rows = slice(part * tm // PROJ_ROW_SPLIT, (part + 1) * tm // PROJ_ROW_SPLIT)
            acc = jnp.dot(h_ref[rows, :], w_ref[...], preferred_element_type=F32) * cs_ref[...]
            mg_ref[rows, :] = jnp.where(is_gate, 0.5 * jnp.tanh(0.5 * acc) + 0.5, acc).astype(BF16)


def _proj(h, w_cat, col_scale, layer, batch, seq_len, tm=1024):
    n, d = h.shape
    tn = PROJ_TILE
    tm = min(tm, seq_len)
    tps = seq_len // tm
    a_shapes, a_specs = [], []
    for _, dil in A_GROUPS:
        a_shapes.append(jax.ShapeDtypeStruct((batch, dil, seq_len // dil, tn), BF16))
        a_specs.append(pl.BlockSpec((None, dil, tm // dil, tn), lambda i, j: (i // tps, 0, i % tps, 0)))
    n_mg = N_GATE_TILES + N_MIX_TILES
    kern = functools.partial(_proj_kernel, tm=tm)
    return pl.pallas_call(
        kern,
        out_shape=(*a_shapes, jax.ShapeDtypeStruct((n, n_mg * tn), BF16)),
        grid=(n // tm, N_A_TILES + n_mg),
        in_specs=[
            pl.BlockSpec((tm, d), lambda i, j: (i, 0)),
            pl.BlockSpec((None, d, tn), lambda i, j: (layer, 0, j)),
            pl.BlockSpec((1, tn), lambda i, j: (0, j)),
        ],
        out_specs=(*a_specs,
                   pl.BlockSpec((tm, tn), lambda i, j: (i, jnp.clip(j - N_A_TILES, 0, n_mg - 1)))),
        scratch_shapes=[pltpu.VMEM((tn // LANES, tm, LANES), F32)],
        compiler_params=_params(("arbitrary", "arbitrary"), 48),
        name="proj",
    )(h, w_cat, col_scale)


def _proj_weights(w_in, w_branch_gate):
    cols = []
    for gi in range(len(A_GROUPS)):
        for part in range(3):
            start = part * (A_COLS // 3) + gi * A_GROUP_COLS
            cols.append(np.arange(start, start + A_GROUP_COLS))
    perm_a = np.concatenate(cols)
    c0 = A_COLS + B_COLS
    w_cat = jnp.concatenate([w_in[..., perm_a], w_branch_gate, w_in[..., c0:c0 + C_COLS],
                             w_in[..., A_COLS:A_COLS + B_COLS]], axis=-1).astype(BF16)

    scale = np.ones((2 * IN_COLS,), np.float32)
    for gi in range(len(A_GROUPS)):
        scale[gi * A_TILE:gi * A_TILE + A_GROUP_COLS] = A_HEAD_DIM ** -0.5 * LOG2E
    qc = A_COLS + MG_C_Q
    scale[qc:qc + C_OUT] = C_HEAD_DIM ** -0.5 * LOG2E
    qb = A_COLS + MG_B_Q
    scale[qb:qb + B_OUT] = B_QK_DIM ** -0.5 * LOG2E
    return w_cat, jnp.asarray(scale).reshape(1, -1)


def _position_features(seq_len):
    j = np.arange(seq_len)
    feat = np.zeros((seq_len, LANES), np.float32)
    feat[:, 0:N_PIECES] = ((j // POS_SPLIT) * POS_SPLIT)[:, None]
    feat[:, N_PIECES:2 * N_PIECES] = (j % POS_SPLIT)[:, None]
    feat[:, 2 * N_PIECES:3 * N_PIECES] = 1.0
    return jnp.asarray(feat, BF16)


ONES_COL = B_V_DIM + 2 * N_PIECES


def _battn_kernel(slope_ref, q_ref, k_ref, v_ref, feat_ref, lq1_ref, lk1_ref, lq2_ref, lk2_ref,
                  gs_ref, o_ref, *, tq, tk, seq_len, lam_init):
    h = pl.program_id(1)
    qi = pl.program_id(2)
    slope = slope_ref[h] * LOG2E
    q = q_ref[...]
    lane = lax.broadcasted_iota(jnp.int32, (tq, LANES), 1)
    zero = jnp.zeros_like(q)
    qq = jnp.concatenate([jnp.where(lane < B_QK_DIM, q, zero),
                          jnp.where(lane >= B_QK_DIM, q, zero)], axis=0)
    row = lax.broadcasted_iota(jnp.int32, (2 * tq, 1), 0)
    qpos = qi * tq + jnp.where(row >= tq, row - tq, row)

    lane2 = lax.broadcasted_iota(jnp.int32, (2 * tq, LANES), 1)
    piece = lane2 % N_PIECES
    s_hi, s_mid, s_lo = _split3(jnp.full((1, LANES), slope, F32))
    c_hi, c_mid, c_lo = _split3(slope * qpos.astype(F32))
    s_piece = jnp.where(piece == 0, s_hi, jnp.where(piece == 1, s_mid, s_lo))
    c_piece = jnp.where(piece == 0, c_hi, jnp.where(piece == 1, c_mid, c_lo))
    aug = jnp.where(lane2 < 2 * N_PIECES, -s_piece, jnp.where(lane2 < 3 * N_PIECES, c_piece, 0.0))
    lhs_right = jnp.concatenate([qq, aug.astype(BF16)], axis=1)
    lhs_left = jnp.concatenate([qq, (-aug).astype(BF16)], axis=1)

    nchunks = seq_len // tk
    jd = (qi * tq) // tk

    def chunk(t, carry, diag):
        m, acc = carry
        j = lax.rem(jd + t, nchunks)
        k0 = pl.multiple_of(j * tk, tk)
        kc = k_ref[pl.ds(k0, tk), :]
        fc = feat_ref[pl.ds(k0, tk), :]
        vf = jnp.concatenate([v_ref[pl.ds(k0, tk), :], fc], axis=1)
        if diag:
            s = lax.dot_general(qq, kc, _NT_DIMS, preferred_element_type=F32)
            kpos = k0 + lax.broadcasted_iota(jnp.int32, (1, tk), 1)
            s = s - slope * jnp.abs(qpos - kpos).astype(F32)
        else:
            lhs = jnp.where(j < jd, lhs_left, lhs_right)
            s = lax.dot_general(lhs, jnp.concatenate([kc, fc], axis=1), _NT_DIMS,
                                preferred_element_type=F32)
        m_new = jnp.maximum(m, jnp.max(s, axis=-1, keepdims=True))
        alpha = jnp.exp2(m - m_new)
        p = jnp.exp2(s - m_new).astype(BF16)
        acc = alpha * acc + jnp.dot(p, vf, preferred_element_type=F32)
        return m_new, acc

    carry = (jnp.full((2 * tq, 1), NEG_INF, F32), jnp.zeros((2 * tq, 2 * LANES), F32))
    carry = chunk(0, carry, True)
    for t in range(1, nchunks):
        carry = chunk(t, carry, False)
    _, acc = carry

    o = acc[:, :B_V_DIM] / acc[:, ONES_COL:ONES_COL + 1]
    lam = (jnp.exp(jnp.sum(lq1_ref[...] * lk1_ref[...], axis=-1, keepdims=True))
           - jnp.exp(jnp.sum(lq2_ref[...] * lk2_ref[...], axis=-1, keepdims=True)) + lam_init)
    diff = o[:tq] - lam * o[tq:]
    ms = jnp.mean(diff * diff, axis=-1, keepdims=True)
    y = diff * lax.rsqrt(ms + RMS_EPS) * gs_ref[...]
    o_ref[...] = (y * (1.0 - lam_init)).astype(BF16)


def _diff_attention(mix, feat, slopes_b, lq1, lk1, lq2, lk2, g_subln, layer, batch, seq_len,
                    tq=512, tk=512):
    n = mix.shape[0]
    tq = min(tq, seq_len)
    tk = min(tk, seq_len)
    nq = seq_len // tq
    qcol, kcol, vcol = (c // B_V_DIM for c in (MG_B_Q, MG_B_K, MG_B_V))
    lam_init = 0.8 - 0.6 * math.exp(-0.3 * layer)
    kern = functools.partial(_battn_kernel, tq=tq, tk=tk, seq_len=seq_len, lam_init=lam_init)
    vec = lambda width: pl.BlockSpec((None, 1, width), lambda b, h, i: (layer, 0, 0))
    return pl.pallas_call(
        kern,
        out_shape=jax.ShapeDtypeStruct((n, B_OUT), BF16),
        grid=(batch, B_HEADS, nq),
        in_specs=[
            pl.BlockSpec(memory_space=pltpu.SMEM),
            pl.BlockSpec((tq, B_V_DIM), lambda b, h, i: (b * nq + i, qcol + h)),
            pl.BlockSpec((seq_len, B_V_DIM), lambda b, h, i: (b, kcol + h)),
            pl.BlockSpec((seq_len, B_V_DIM), lambda b, h, i: (b, vcol + h)),
            pl.BlockSpec((seq_len, LANES), lambda b, h, i: (0, 0)),
            vec(B_QK_DIM), vec(B_QK_DIM), vec(B_QK_DIM), vec(B_QK_DIM), vec(B_V_DIM),
        ],
        out_specs=pl.BlockSpec((tq, B_V_DIM), lambda b, h, i: (b * nq + i, h)),
        compiler_params=_params(("arbitrary", "arbitrary", "arbitrary"), 48),
        name="diff_attention",
    )(slopes_b, mix, mix, mix, feat, lq1, lk1, lq2, lk2, g_subln)


def _band_window(qi, tq, half, length):
    kw = tq + 2 * half
    ks = jnp.clip(qi * tq - half, 0, length - kw)
    ks = pl.multiple_of(ks, half)
    qpos = qi * tq + lax.broadcasted_iota(jnp.int32, (tq, 1), 0)
    kpos = ks + lax.broadcasted_iota(jnp.int32, (1, kw), 1)
    rel = jnp.abs(qpos - kpos)
    return ks, kw, rel <= half, rel.astype(F32)


def _cattn_kernel(slope_ref, sink_ref, q_ref, k_ref, v_ref, o_ref, *, tq, seq_len):
    g = pl.program_id(1)
    qi = pl.program_id(2)
    ks, kw, valid, relf = _band_window(qi, tq, C_HALF_WINDOW, seq_len)
    kwin = k_ref[pl.ds(ks, kw), :]
    vwin = v_ref[pl.ds(ks, kw), :]
    for hh in range(C_GROUP):
        head = g * C_GROUP + hh
        slope = slope_ref[head] * LOG2E
        sink = sink_ref[head] * LOG2E
        cols = slice(hh * C_HEAD_DIM, (hh + 1) * C_HEAD_DIM)
        s = lax.dot_general(q_ref[:, cols], kwin, _NT_DIMS, preferred_element_type=F32)
        s = jnp.where(valid, s - slope * relf, NEG_INF)
        m = jnp.maximum(jnp.max(s, axis=-1, keepdims=True), sink)
        p = jnp.exp2(s - m)
        den = jnp.sum(p, axis=-1, keepdims=True) + jnp.exp2(sink - m)
        o = jnp.dot(p.astype(BF16), vwin, preferred_element_type=F32) / den
        o_ref[:, cols] = o.astype(BF16)


def _window_attention(mix, slopes_c, sink, batch, seq_len, tq=256):
    n = mix.shape[0]
    nq = seq_len // tq
    gw = C_GROUP * C_HEAD_DIM
    qcol = MG_C_Q // gw
    kcol = MG_C_K // C_HEAD_DIM
    vcol = MG_C_V // C_HEAD_DIM
    kern = functools.partial(_cattn_kernel, tq=tq, seq_len=seq_len)
    return pl.pallas_call(
        kern,
        out_shape=jax.ShapeDtypeStruct((n, C_OUT), BF16),
        grid=(batch, C_KV_HEADS, nq),
        in_specs=[
            pl.BlockSpec(memory_space=pltpu.SMEM),
            pl.BlockSpec(memory_space=pltpu.SMEM),
            pl.BlockSpec((tq, gw), lambda b, g, i: (b * nq + i, qcol + g)),
            pl.BlockSpec((seq_len, C_HEAD_DIM), lambda b, g, i: (b, kcol + g)),
            pl.BlockSpec((seq_len, C_HEAD_DIM), lambda b, g, i: (b, vcol + g)),
        ],
        out_specs=pl.BlockSpec((tq, gw), lambda b, g, i: (b * nq + i, g)),
        compiler_params=_params(("arbitrary", "arbitrary", "arbitrary"), 32),
        name="window_attention",
    )(slopes_c, sink, mix, mix, mix)


def _aattn_kernel(q_ref, k_ref, v_ref, o_ref, lse_ref, *scratch, tq, length, slopes, dil):
    qi = pl.program_id(1)
    nslab = A_GROUP_COLS // LANES
    half = A_GROUPS[0][0] // 2
    ks, kw, valid, relf = _band_window(qi, tq, half, length)
    lane = lax.broadcasted_iota(jnp.int32, (tq, A_GROUP_COLS), 1)

    def residue(r, _):
        kwin = k_ref[r, pl.ds(ks, kw), :]
        vwin = v_ref[r, pl.ds(ks, kw), :]
        q = q_ref[r]
        zero = jnp.zeros_like(q)
        out = jnp.zeros((tq, A_GROUP_COLS), F32)
        lse_out = jnp.zeros((tq, A_GROUP_COLS), F32)
        for hh in range(A_HEADS_PER_GROUP):
            in_head = (lane >= hh * A_HEAD_DIM) & (lane < (hh + 1) * A_HEAD_DIM)
            s = lax.dot_general(jnp.where(in_head, q, zero), kwin, _NT_DIMS,
                                preferred_element_type=F32)
            s = jnp.where(valid, s - (slopes[hh] * dil * LOG2E) * relf, NEG_INF)
            m = jnp.max(s, axis=-1, keepdims=True)
            p = jnp.exp2(s - m)
            den = jnp.sum(p, axis=-1, keepdims=True)
            o = jnp.dot(p.astype(BF16), vwin, preferred_element_type=F32) / den
            out = jnp.where(in_head, o, out)
            lse_out = jnp.where(in_head, m + jnp.log2(den), lse_out)
        if dil == 1:
            o_ref[...] = out
            lse_ref[...] = lse_out
        else:
            o_scr, lse_scr = scratch
            for c in range(nslab):
                cols = slice(c * LANES, (c + 1) * LANES)
                o_scr[c, pl.ds(r, tq, stride=dil), :] = out[:, cols]
                lse_scr[c, pl.ds(r, tq, stride=dil), :] = lse_out[:, cols]
        return 0

    if dil == 1:
        residue(0, 0)
    else:
        lax.fori_loop(0, dil, residue, 0)
        o_scr, lse_scr = scratch
        o_ref[...] = jnp.concatenate([o_scr[c] for c in range(nslab)], axis=1)
        lse_ref[...] = jnp.concatenate([lse_scr[c] for c in range(nslab)], axis=1)


def _dilated_attention(qkv, group, slopes, batch, seq_len, tq=256):
    window, dil = A_GROUPS[group]
    half = window // (2 * dil)
    length = seq_len // dil
    tq = min(tq, length - 2 * half)
    nq = length // tq
    n = batch * seq_len
    kern = functools.partial(_aattn_kernel, tq=tq, length=length,
                             slopes=tuple(float(v) for v in slopes), dil=dil)
    out_sds = jax.ShapeDtypeStruct((n, A_GROUP_COLS), F32)
    out_spec = pl.BlockSpec((tq * dil, A_GROUP_COLS), lambda b, i: (b * nq + i, 0))
    kv_spec = lambda part: pl.BlockSpec((None, dil, length, A_GROUP_COLS), lambda b, i: (b, 0, 0, part),
                                        pipeline_mode=pl.Buffered(1))
    return pl.pallas_call(
        kern,
        out_shape=(out_sds, out_sds),
        grid=(batch, nq),
        in_specs=[
            pl.BlockSpec((None, dil, tq, A_GROUP_COLS), lambda b, i: (b, 0, i, 0)),
            kv_spec(1), kv_spec(2),
        ],
        out_specs=(out_spec, out_spec),
        scratch_shapes=[] if dil == 1 else
        [pltpu.VMEM((A_GROUP_COLS // LANES, tq * dil, LANES), F32)] * 2,
        compiler_params=_params(("arbitrary", "arbitrary"), 48),
        name=f"dilated_attention_g{group}",
    )(qkv, qkv, qkv)


def _merge_kernel(xa_ref, xb_ref, gt_ref, o0_ref, o1_ref, o2_ref, l0_ref, l1_ref, l2_ref, ob_ref, oc_ref,
                  ga_ref, gb_ref, gc_ref, wa_ref, wb_ref, wc_ref, wo_ref, g2_ref, sc2_ref, sh2_ref,
                  wr_ref, out_ref, h_ref, aff_ref, *, split):
    l0, l1, l2 = l0_ref[...], l1_ref[...], l2_ref[...]
    m = jnp.maximum(jnp.maximum(l0, l1), l2)
    e0, e1, e2 = jnp.exp2(l0 - m), jnp.exp2(l1 - m), jnp.exp2(l2 - m)
    oa = (e0 * o0_ref[...] + e1 * o1_ref[...] + e2 * o2_ref[...]) / (e0 + e1 + e2)
    ya = jnp.dot(oa.astype(BF16), wa_ref[...], preferred_element_type=F32)
    yb = jnp.dot(ob_ref[...], wb_ref[...], preferred_element_type=F32)
    yc = jnp.dot(oc_ref[...], wc_ref[...], preferred_element_type=F32)
    merged = (ga_ref[...].astype(F32) * ya + gb_ref[...].astype(F32) * yb
              + gc_ref[...].astype(F32) * yc)
    y = jnp.dot(merged.astype(BF16), wo_ref[...], preferred_element_type=F32)
    x = _two_part_load(xa_ref, xb_ref, split) + gt_ref[...] * y
    out_ref[...] = x

    h = _rms_mod(x, g2_ref[...], sc2_ref[...], sh2_ref[...])
    h_ref[...] = h.astype(BF16)
    w = wr_ref[...]
    h_hi = h.astype(BF16)
    h_lo = (h - h_hi.astype(F32)).astype(BF16)
    w_hi = w.astype(BF16)
    w_lo = (w - w_hi.astype(F32)).astype(BF16)
    logits = (jnp.dot(h_hi, w_hi, preferred_element_type=F32)
              + jnp.dot(h_lo, w_hi, preferred_element_type=F32)
              + jnp.dot(h_hi, w_lo, preferred_element_type=F32))
    mx = jnp.max(logits, axis=-1, keepdims=True)
    e = jnp.exp(logits - mx)
    aff_ref[...] = e / jnp.sum(e, axis=-1, keepdims=True)


def _merge(xa, xb, mod, oa_parts, ob, oc, mg, w_br_a, w_br_b, w_br_c, w_o, g2, w_router, layer,
           seq_len, tm=256):
    d = xa.shape[1]
    n = mg.shape[0]
    tm = min(tm, seq_len)
    split = min(xa.shape[0], n) // tm
    tiles_per_seq = seq_len // tm
    seq_of = lambda i: i // tiles_per_seq
    ne = w_router.shape[-1]
    (o0, l0), (o1, l1), (o2, l2) = oa_parts
    row = lambda width: pl.BlockSpec((tm, width), lambda i: (i, 0))
    resident = lambda rows: pl.BlockSpec((None, rows, d), lambda i: (layer, 0, 0),
                                         pipeline_mode=pl.Buffered(1))
    gate = lambda k: pl.BlockSpec((tm, d), lambda i: (i, k))
    modrow = lambda chunk: pl.BlockSpec((None, None, None, 1, d), _mod_spec(layer, chunk, seq_of))
    return pl.pallas_call(
        functools.partial(_merge_kernel, split=split),
        out_shape=(jax.ShapeDtypeStruct((n, d), F32), jax.ShapeDtypeStruct((n, d), BF16),
                   jax.ShapeDtypeStruct((n, ne), F32)),
        grid=(n // tm,),
        in_specs=[
            *_two_part_specs(tm, d, split), modrow(2),
            row(A_OUT), row(A_OUT), row(A_OUT), row(A_OUT), row(A_OUT), row(A_OUT),
            row(B_OUT), row(C_OUT),
            gate(0), gate(1), gate(2),
            resident(A_OUT), resident(B_OUT), resident(C_OUT), resident(d),
            pl.BlockSpec((None, 1, d), lambda i: (layer, 0, 0)), modrow(4), modrow(3),
            pl.BlockSpec((None, d, ne), lambda i: (layer, 0, 0)),
        ],
        out_specs=(row(d), row(d), row(ne)),
        compiler_params=_params(("arbitrary",), 56),
        name="merge",
    )(xa, xb, mod, o0, o1, o2, l0, l1, l2, ob, oc, mg, mg, mg, w_br_a, w_br_b, w_br_c, w_o,
      g2, mod, mod, w_router)


def _ffn_kernel(x_ref, gate_ref, wg_ref, wu_ref, wd_ref, o_ref, acc_scr):
    f = pl.program_id(2)

    @pl.when(f == 0)
    def _():
        acc_scr[...] = jnp.zeros_like(acc_scr)

    x = x_ref[...]
    a = jnp.dot(x, wg_ref[...], preferred_element_type=F32)
    u = jnp.dot(x, wu_ref[...], preferred_element_type=F32)
    hid = (a * jax.nn.sigmoid(a) * u).astype(BF16)
    acc_scr[...] += jnp.dot(hid, wd_ref[...], preferred_element_type=F32)

    @pl.when(f == pl.num_programs(2) - 1)
    def _():
        o_ref[...] = (acc_scr[...] * gate_ref[...]).astype(BF16)


def _expert_ffn(xe, gate, w_gate, w_up, w_down, layer, tm=1024, tf=512):
    ne, cap, d = xe.shape
    dff = w_gate.shape[-1]
    tm = min(tm, cap)
    return pl.pallas_call(
        _ffn_kernel,
        out_shape=jax.ShapeDtypeStruct((ne, cap, d), BF16),
        grid=(ne, cap // tm, dff // tf),
        in_specs=[
            pl.BlockSpec((None, tm, d), lambda e, i, f: (e, i, 0)),
            pl.BlockSpec((None, tm, 1), lambda e, i, f: (e, i, 0)),
            pl.BlockSpec((None, None, d, tf), lambda e, i, f: (layer, e, 0, f)),
            pl.BlockSpec((None, None, d, tf), lambda e, i, f: (layer, e, 0, f)),
            pl.BlockSpec((None, None, tf, d), lambda e, i, f: (layer, e, f, 0)),
        ],
        out_specs=pl.BlockSpec((None, tm, d), lambda e, i, f: (e, i, 0)),
        scratch_shapes=[pltpu.VMEM((tm, d), F32)],
        compiler_params=_params(("arbitrary", "arbitrary", "arbitrary"), 48),
        name="expert_ffn",
    )(xe, gate, w_gate, w_up, w_down)


WINDOW = 128
PAIR = 2 * WINDOW
N_PAIR_BUFS = 4


def _combine_copy(ye_hbm, buf, sem, expert, window, slot, half):
    start = pl.multiple_of(window * WINDOW, WINDOW)
    return pltpu.make_async_copy(ye_hbm.at[expert, pl.ds(start, WINDOW), :],
                                 buf.at[slot, pl.ds(half * WINDOW, WINDOW), :], sem.at[slot, half])


def _combine_kernel(cnt_ref, we_ref, ww_ref, idx_ref, x_ref, gt_ref, g_ref, sc_ref, sh_ref, ye_hbm,
                    o_ref, o2_ref, buf, sem, acc_scr, *, tm, final_norm, split):
    tile = pl.program_id(0)
    n = cnt_ref[tile]
    npairs = (n + 1) // 2
    tokens = tile * tm + lax.broadcasted_iota(jnp.int32, (tm, WINDOW), 0)
    acc_scr[...] = jnp.zeros_like(acc_scr)

    @pl.when(tile == 0)
    def _():
        buf[...] = jnp.zeros_like(buf)

    def copy(q, half):
        p = 2 * q + half
        return _combine_copy(ye_hbm, buf, sem, we_ref[tile, p], ww_ref[tile, p],
                             lax.rem(q, N_PAIR_BUFS), half)

    def start_pair(q):
        for half in range(2):
            @pl.when(2 * q + half < n)
            def _(half=half):
                copy(q, half).start()

    for ahead in range(N_PAIR_BUFS - 1):
        @pl.when(ahead < npairs)
        def _(ahead=ahead):
            start_pair(ahead)

    def body(q, _):
        for half in range(2):
            @pl.when(2 * q + half < n)
            def _(half=half):
                copy(q, half).wait()

        @pl.when(q + N_PAIR_BUFS - 1 < npairs)
        def _():
            start_pair(q + N_PAIR_BUFS - 1)

        blocks = []
        for half in range(2):
            p = 2 * q + half
            ids = idx_ref[we_ref[tile, p], ww_ref[tile, p]]
            blocks.append(jnp.where((ids == tokens) & (p < n), 1.0, 0.0).astype(BF16))
        onehot = jnp.concatenate(blocks, axis=1)
        acc_scr[...] += jnp.dot(onehot, buf[lax.rem(q, N_PAIR_BUFS)], preferred_element_type=F32)
        return 0

    lax.fori_loop(0, npairs, body, 0)
    x = x_ref[...] + gt_ref[...] * acc_scr[...]
    if final_norm:
        ms = jnp.mean(x * x, axis=-1, keepdims=True)
        y = x * lax.rsqrt(ms + RMS_EPS) * g_ref[...]

        @pl.when(tile < split)
        def _():
            o_ref[...] = y

        @pl.when(tile >= split)
        def _():
            o2_ref[...] = y
    else:
        o_ref[...] = x
        o2_ref[...] = _rms_mod(x, g_ref[...], sc_ref[...], sh_ref[...]).astype(BF16)


def _combine_plan(idx_sorted, n_tokens, tm):
    ne, cap = idx_sorted.shape
    nt = n_tokens // tm
    bounds = jnp.arange(nt + 1, dtype=jnp.int32) * tm
    base = jnp.sum(idx_sorted[:, :, None] < bounds[None, None, :], axis=1, dtype=jnp.int32)
    lo, hi = base[:, :-1], base[:, 1:]
    first = lo // WINDOW
    nwin = jnp.where(hi > lo, (hi - 1) // WINDOW - first + 1, 0)
    cum_incl = jnp.cumsum(nwin, axis=0)
    cum_excl = cum_incl - nwin
    count = cum_incl[-1]
    max_pairs = ne * (tm // WINDOW + 1) + 1
    p = jnp.arange(max_pairs, dtype=jnp.int32)
    expert = jnp.sum(cum_incl.T[:, None, :] <= p[None, :, None], axis=-1, dtype=jnp.int32)
    expert = jnp.minimum(expert, ne - 1)
    k = p[None, :] - jnp.take_along_axis(cum_excl.T, expert, axis=1)
    window = jnp.take_along_axis(first.T, expert, axis=1) + k
    window = jnp.clip(window, 0, cap // WINDOW - 1)
    return count.astype(jnp.int32), expert, window.astype(jnp.int32)


def _combine(x, ye, idx_sorted, mod, g_next, layer, seq_len, final_norm, n_first=0, tm=512):
    n, d = x.shape
    tm = min(tm, seq_len)
    tiles_per_seq = seq_len // tm
    seq_of = lambda i, *_: i // tiles_per_seq
    count, expert, window = _combine_plan(idx_sorted, n, tm)
    ne, cap = idx_sorted.shape
    idx_windows = idx_sorted.reshape(ne, cap // WINDOW, 1, WINDOW)
    split = n_first // tm
    kern = functools.partial(_combine_kernel, tm=tm, final_norm=final_norm, split=split)
    row = pl.BlockSpec((tm, d), lambda i, *_: (i, 0))
    modrow = lambda lyr, chunk: pl.BlockSpec((None, None, None, 1, d), _mod_spec(lyr, chunk, seq_of))
    if final_norm:
        g_spec = pl.BlockSpec((1, d), lambda i, *_: (0, 0))
        norm_specs = [g_spec, modrow(layer, 1), modrow(layer, 0)]
        out_shape = (jax.ShapeDtypeStruct((n_first, d), F32),
                     jax.ShapeDtypeStruct((n - n_first, d), F32))
        out_specs = _two_part_specs(tm, d, split)
    else:
        g_spec = pl.BlockSpec((None, 1, d), lambda i, *_: (layer + 1, 0, 0))
        norm_specs = [g_spec, modrow(layer + 1, 1), modrow(layer + 1, 0)]
        out_shape = (jax.ShapeDtypeStruct((n, d), F32), jax.ShapeDtypeStruct((n, d), BF16))
        out_specs = (row, row)
    grid_spec = pltpu.PrefetchScalarGridSpec(
        num_scalar_prefetch=3,
        grid=(n // tm,),
        in_specs=[
            pl.BlockSpec(idx_windows.shape, lambda i, *_: (0, 0, 0, 0)),
            row,
            modrow(layer, N_MOD - 1),
            *norm_specs,
            pl.BlockSpec(memory_space=pl.ANY),
        ],
        out_specs=out_specs,
        scratch_shapes=[pltpu.VMEM((N_PAIR_BUFS, PAIR, d), BF16),
                        pltpu.SemaphoreType.DMA((N_PAIR_BUFS, 2)),
                        pltpu.VMEM((tm, d), F32)],
    )
    return pl.pallas_call(
        kern,
        out_shape=out_shape,
        grid_spec=grid_spec,
        compiler_params=_params(("arbitrary",), 40),
        name="combine",
    )(count, expert, window, idx_windows, x, mod, g_next, mod, mod, ye)


def _trunk(xa, xb, c_rows, seq_len, g_norm1, g_norm2, w_ada, b_ada, w_in, w_branch_gate,
           w_br_a, w_br_b, w_br_c, w_o, lambda_q1, lambda_k1, lambda_q2, lambda_k2, g_subln, sink,
           w_router, w_e_gate, w_e_up, w_e_down, g_final):
    d = xa.shape[1]
    group_sizes = (xa.shape[0], xb.shape[0])
    n = sum(group_sizes)
    depth = w_in.shape[0]
    batch = n // seq_len
    s_a, s_b, s_c = _alibi_slopes()

    w_cat, col_scale = _proj_weights(w_in, w_branch_gate)
    wa, wb, wc, wo = (w.astype(BF16) for w in (w_br_a, w_br_b, w_br_c, w_o))
    weg, weu, wed = (w.astype(BF16) for w in (w_e_gate, w_e_up, w_e_down))
    row3 = lambda a: a.reshape(depth, 1, a.shape[-1])
    g1, g2 = row3(g_norm1), row3(g_norm2)
    lq1, lk1, lq2, lk2, gsub = (row3(a) for a in (lambda_q1, lambda_k1, lambda_q2, lambda_k2, g_subln))
    feat = _position_features(seq_len)

    mod = _ada_modulation(c_rows, w_ada, b_ada)
    mod = mod.reshape(depth, c_rows.shape[0], N_MOD, 1, d)

    h1 = _norm_mod(xa, xb, g1, mod, 0, seq_len)
    for layer in range(depth):
        *a_qkv, mg = _proj(h1, w_cat, col_scale, layer, batch, seq_len)
        oa_parts = [
            _dilated_attention(a_qkv[gi], gi,
                               s_a[gi * A_HEADS_PER_GROUP:(gi + 1) * A_HEADS_PER_GROUP],
                               batch, seq_len)
            for gi in range(len(A_GROUPS))
        ]
        ob = _diff_attention(mg, feat, jnp.asarray(s_b), lq1, lk1, lq2, lk2, gsub, layer, batch,
                             seq_len)
        oc = _window_attention(mg, jnp.asarray(s_c), sink[layer], batch, seq_len)
        x, h2, aff = _merge(xa, xb, mod, oa_parts, ob, oc, mg, wa, wb, wc, wo, g2, w_router, layer,
                            seq_len)
        aff_t = aff.T

        idx_parts, gate_parts = [], []
        start = 0
        for size in group_sizes:
            cap = (EC_CAPACITY_FACTOR * size) // N_EXPERTS
            gate, idx = lax.top_k(aff_t[:, start:start + size], cap)
            idx_parts.append(idx + start)
            gate_parts.append(gate)
            start += size
        idx = jnp.concatenate(idx_parts, axis=1)
        gate = jnp.concatenate(gate_parts, axis=1)
        idx, gate = lax.sort((idx, gate), dimension=1, num_keys=1)
        xe = jnp.take(h2, idx, axis=0, mode="clip")
        ye = _expert_ffn(xe, gate[..., None], weg, weu, wed, layer)
        if layer == depth - 1:
            return _combine(x, ye, idx, mod, g_final.reshape(1, d), layer, seq_len, True,
                            n_first=group_sizes[0])
        xa, h1 = _combine(x, ye, idx, mod, g1, layer, seq_len, False)
        xb = xa


def kernel(x_prompt, x_sample, c_prompt, c_sample, g_norm1, g_norm2, w_ada, b_ada, w_in, w_branch_gate, w_br_a, w_br_b, w_br_c, w_o, lambda_q1, lambda_k1, lambda_q2, lambda_k2, g_subln, sink, w_router, w_e_gate, w_e_up, w_e_down, g_final):
    bp, seq_len, d = x_prompt.shape
    bs = x_sample.shape[0]
    assert x_sample.shape[1] == seq_len
    c = jnp.concatenate([c_prompt, c_sample], axis=0)
    pad = -c.shape[0] % SUBLANES
    c_rows = jnp.pad(c, ((0, pad), (0, 0)))
    y_prompt, y_sample = _trunk(
        x_prompt.reshape(bp * seq_len, d), x_sample.reshape(bs * seq_len, d), c_rows, seq_len,
        g_norm1, g_norm2, w_ada, b_ada, w_in, w_branch_gate, w_br_a, w_br_b, w_br_c, w_o,
        lambda_q1, lambda_k1, lambda_q2, lambda_k2, g_subln, sink, w_router, w_e_gate, w_e_up,
        w_e_down, g_final)
    return (y_prompt.reshape(bp, seq_len, d), y_sample.reshape(bs, seq_len, d))
```

```python
import functools
import math

import numpy as np
import jax
import jax.numpy as jnp
from jax import lax
from jax.experimental import pallas as pl
from jax.experimental.pallas import tpu as pltpu

F32 = jnp.float32
BF16 = jnp.bfloat16

A_GROUPS = ((128, 1), (512, 4), (2048, 16))
A_HEADS_PER_GROUP = 4
A_HEAD_DIM = 64
A_N_HEADS = 12
A_OUT = 256
A_GROUP_COLS = A_HEADS_PER_GROUP * A_HEAD_DIM
A_TILE = 3 * A_GROUP_COLS
B_HEADS = 6
B_QK_DIM = 64
B_V_DIM = 128
B_OUT = 768
C_Q_HEADS = 8
C_KV_HEADS = 2
C_GROUP = C_Q_HEADS // C_KV_HEADS
C_HEAD_DIM = 128
C_HALF_WINDOW = 128
C_OUT = 1024
A_COLS = 2304
B_COLS = 2304
C_COLS = 1536
IN_COLS = 6144
N_BRANCH = 3
N_EXPERTS = 16
EC_CAPACITY_FACTOR = 2
N_ALIBI_HEADS = 26
RMS_EPS = 1e-6
NEG_INF = -1e30
LOG2E = math.log2(math.e)
N_MOD = 6
SUBLANES = 8
LANES = 128
POS_SPLIT = 64
N_PIECES = 3

_NT_DIMS = (((1,), (1,)), ((), ()))
_MIB = 1024 * 1024


def _alibi_slopes():
    n = N_ALIBI_HEADS
    s = 2.0 ** (-8.0 * np.arange(1, n + 1, dtype=np.float32) / n)
    s = s.astype(np.float32)
    s_c = s[:C_Q_HEADS]
    s_a = s[C_Q_HEADS:C_Q_HEADS + A_N_HEADS]
    s_b = s[C_Q_HEADS + A_N_HEADS:]
    return s_a, s_b, s_c


def _params(semantics, vmem_mib):
    return pltpu.CompilerParams(dimension_semantics=semantics,
                                vmem_limit_bytes=vmem_mib * _MIB)


def _mod_spec(layer, chunk, seq_of):
    def index(*ids):
        return (layer, seq_of(*ids), chunk, 0, 0)
    return index


def _rms_mod(x, g, scale, shift):
    ms = jnp.mean(x * x, axis=-1, keepdims=True)
    y = x * lax.rsqrt(ms + RMS_EPS) * g
    return y * (1.0 + scale) + shift


def _split3(x):
    hi = x.astype(BF16).astype(F32)
    mid = (x - hi).astype(BF16).astype(F32)
    lo = (x - hi - mid).astype(BF16).astype(F32)
    return hi, mid, lo


def _ada_kernel(c_ref, w_ref, b_ref, o_ref):
    c = c_ref[...]
    a = c * jax.nn.sigmoid(c)
    o_ref[...] = jnp.dot(a, w_ref[...], precision=lax.Precision.HIGHEST,
                         preferred_element_type=F32) + b_ref[...]


def _ada_modulation(c_rows, w_ada, b_ada, tn=1024):
    depth, d, n6 = w_ada.shape
    rows = c_rows.shape[0]
    return pl.pallas_call(
        _ada_kernel,
        out_shape=jax.ShapeDtypeStruct((depth, rows, n6), F32),
        grid=(depth, n6 // tn),
        in_specs=[
            pl.BlockSpec((rows, d), lambda l, j: (0, 0)),
            pl.BlockSpec((None, d, tn), lambda l, j: (l, 0, j)),
            pl.BlockSpec((None, 1, tn), lambda l, j: (l, 0, j)),
        ],
        out_specs=pl.BlockSpec((None, rows, tn), lambda l, j: (l, 0, j)),
        compiler_params=_params(("arbitrary", "arbitrary"), 40),
        name="ada_modulation",
    )(c_rows, w_ada, b_ada.reshape(depth, 1, n6))


def _two_part_specs(tm, d, split):
    first = pl.BlockSpec((tm, d), lambda i, *_: (jnp.minimum(i, split - 1), 0))
    second = pl.BlockSpec((tm, d), lambda i, *_: (jnp.maximum(i - split, 0), 0))
    return first, second


def _two_part_load(xa_ref, xb_ref, split):
    return jnp.where(pl.program_id(0) < split, xa_ref[...], xb_ref[...])


def _norm_kernel(xa_ref, xb_ref, g_ref, sc_ref, sh_ref, h_ref, *, split):
    x = _two_part_load(xa_ref, xb_ref, split)
    h_ref[...] = _rms_mod(x, g_ref[...], sc_ref[...], sh_ref[...]).astype(BF16)


def _norm_mod(xa, xb, g, mod, layer, seq_len, tm=512):
    d = xa.shape[1]
    n = xa.shape[0] + xb.shape[0]
    tm = min(tm, seq_len)
    tps = seq_len // tm
    seq_of = lambda i: i // tps
    split = xa.shape[0] // tm
    return pl.pallas_call(
        functools.partial(_norm_kernel, split=split),
        out_shape=jax.ShapeDtypeStruct((n, d), BF16),
        grid=(n // tm,),
        in_specs=[*_two_part_specs(tm, d, split),
                  pl.BlockSpec((None, 1, d), lambda i: (layer, 0, 0)),
                  pl.BlockSpec((None, None, None, 1, d), _mod_spec(layer, 1, seq_of)),
                  pl.BlockSpec((None, None, None, 1, d), _mod_spec(layer, 0, seq_of))],
        out_specs=pl.BlockSpec((tm, d), lambda i: (i, 0)),
        compiler_params=_params(("arbitrary",), 32),
        name="norm_mod",
    )(xa, xb, g, mod, mod)


PROJ_TILE = A_TILE
N_A_TILES = len(A_GROUPS)
N_GATE_TILES = IN_COLS // PROJ_TILE
N_MIX_TILES = (C_COLS + B_COLS) // PROJ_TILE
PROJ_ROW_SPLIT = 2
MG_C_Q = IN_COLS
MG_C_K = MG_C_Q + C_OUT
MG_C_V = MG_C_K + C_KV_HEADS * C_HEAD_DIM
MG_B_Q = MG_C_Q + C_COLS
MG_B_K = MG_B_Q + B_OUT
MG_B_V = MG_B_K + B_OUT


def _proj_kernel(h_ref, w_ref, cs_ref, a0_ref, a1_ref, a2_ref, mg_ref, acc_scr, *, tm):
    j = pl.program_id(1)
    nslab = PROJ_TILE // LANES

    @pl.when(j < N_A_TILES)
    def _():
        acc = jnp.dot(h_ref[...], w_ref[...], preferred_element_type=F32) * cs_ref[...]

        @pl.when(j == 0)
        def _():
            a0_ref[0] = acc.astype(BF16)

        @pl.when(j > 0)
        def _():
            for c in range(nslab):
                acc_scr[c] = acc[:, c * LANES:(c + 1) * LANES]

        for gi, a_ref in ((1, a1_ref), (2, a2_ref)):
            dil = A_GROUPS[gi][1]

            @pl.when(j == gi)
            def _(a_ref=a_ref, dil=dil):
                for r in range(dil):
                    rows = [acc_scr[c, pl.ds(r, tm // dil, stride=dil), :] for c in range(nslab)]
                    a_ref[r] = jnp.concatenate(rows, axis=1).astype(BF16)

    @pl.when(j >= N_A_TILES)
    def _():
        is_gate = j < N_A_TILES + N_GATE_TILES
        for part in range(PROJ_ROW_SPLIT):
            rows = slice(part * tm // PROJ_ROW_SPLIT, (part + 1) * tm // PROJ_ROW_SPLIT)
            acc = jnp.dot(h_ref[rows, :], w_ref[...], preferred_element_type=F32) * cs_ref[...]
            mg_ref[rows, :] = jnp.where(is_gate, 0.5 * jnp.tanh(0.5 * acc) + 0.5, acc).astype(BF16)


def _proj(h, w_cat, col_scale, layer, batch, seq_len, tm=1024):
    n, d = h.shape
    tn = PROJ_TILE
    tm = min(tm, seq_len)
    tps = seq_len // tm
    a_shapes, a_specs = [], []
    for _, dil in A_GROUPS:
        a_shapes.append(jax.ShapeDtypeStruct((batch, dil, seq_len // dil, tn), BF16))
        a_specs.append(pl.BlockSpec((None, dil, tm // dil, tn), lambda i, j: (i // tps, 0, i % tps, 0)))
    n_mg = N_GATE_TILES + N_MIX_TILES
    kern = functools.partial(_proj_kernel, tm=tm)
    return pl.pallas_call(
        kern,
        out_shape=(*a_shapes, jax.ShapeDtypeStruct((n, n_mg * tn), BF16)),
        grid=(n // tm, N_A_TILES + n_mg),
        in_specs=[
            pl.BlockSpec((tm, d), lambda i, j: (i, 0)),
            pl.BlockSpec((None, d, tn), lambda i, j: (layer, 0, j)),
            pl.BlockSpec((1, tn), lambda i, j: (0, j)),
        ],
        out_specs=(*a_specs,
                   pl.BlockSpec((tm, tn), lambda i, j: (i, jnp.clip(j - N_A_TILES, 0, n_mg - 1)))),
        scratch_shapes=[pltpu.VMEM((tn // LANES, tm, LANES), F32)],
        compiler_params=_params(("arbitrary", "arbitrary"), 48),
        name="proj",
    )(h, w_cat, col_scale)


def _proj_weights(w_in, w_branch_gate):
    cols = []
    for gi in range(len(A_GROUPS)):
        for part in range(3):
            start = part * (A_COLS // 3) + gi * A_GROUP_COLS
            cols.append(np.arange(start, start + A_GROUP_COLS))
    perm_a = np.concatenate(cols)
    c0 = A_COLS + B_COLS
    w_cat = jnp.concatenate([w_in[..., perm_a], w_branch_gate, w_in[..., c0:c0 + C_COLS],
                             w_in[..., A_COLS:A_COLS + B_COLS]], axis=-1).astype(BF16)

    scale = np.ones((2 * IN_COLS,), np.float32)
    for gi in range(len(A_GROUPS)):
        scale[gi * A_TILE:gi * A_TILE + A_GROUP_COLS] = A_HEAD_DIM ** -0.5 * LOG2E
    qc = A_COLS + MG_C_Q
    scale[qc:qc + C_OUT] = C_HEAD_DIM ** -0.5 * LOG2E
    qb = A_COLS + MG_B_Q
    scale[qb:qb + B_OUT] = B_QK_DIM ** -0.5 * LOG2E
    return w_cat, jnp.asarray(scale).reshape(1, -1)


def _position_features(seq_len):
    j = np.arange(seq_len)
    feat = np.zeros((seq_len, LANES), np.float32)
    feat[:, 0:N_PIECES] = ((j // POS_SPLIT) * POS_SPLIT)[:, None]
    feat[:, N_PIECES:2 * N_PIECES] = (j % POS_SPLIT)[:, None]
    feat[:, 2 * N_PIECES:3 * N_PIECES] = 1.0
    return jnp.asarray(feat, BF16)


ONES_COL = B_V_DIM + 2 * N_PIECES


def _alibi_columns(slopes, seq_len):
    slope = (slopes * LOG2E)[:, None]
    pos = jnp.arange(seq_len, dtype=F32)[None, :]
    s3 = [jnp.broadcast_to(-piece, (slopes.shape[0], seq_len)) for piece in _split3(slope)]
    c3 = list(_split3(slope * pos))
    cols = jnp.stack(s3 + s3 + c3, axis=-1)
    return jnp.pad(cols, ((0, 0), (0, 0), (0, LANES - cols.shape[-1]))).astype(BF16)


def _battn_kernel(slope_ref, q_ref, aug_ref, k_ref, v_ref, feat_ref, lq1_ref, lk1_ref, lq2_ref, lk2_ref,
                  gs_ref, o_ref, *, tq, tk, seq_len, lam_init):
    h = pl.program_id(1)
    qi = pl.program_id(2)
    slope = slope_ref[h] * LOG2E
    q = q_ref[...]
    lane = lax.broadcasted_iota(jnp.int32, (tq, LANES), 1)
    zero = jnp.zeros_like(q)
    qq = jnp.concatenate([jnp.where(lane < B_QK_DIM, q, zero),
                          jnp.where(lane >= B_QK_DIM, q, zero)], axis=0)
    row = lax.broadcasted_iota(jnp.int32, (2 * tq, 1), 0)
    qpos = qi * tq + jnp.where(row >= tq, row - tq, row)

    aug = jnp.concatenate([aug_ref[...], aug_ref[...]], axis=0)
    lhs_right = jnp.concatenate([qq, aug], axis=1)
    lhs_left = jnp.concatenate([qq, -aug], axis=1)

    nchunks = seq_len // tk
    jd = (qi * tq) // tk

    def chunk(t, carry, diag):
        m, acc = carry
        j = lax.rem(jd + t, nchunks)
        k0 = pl.multiple_of(j * tk, tk)
        kc = k_ref[pl.ds(k0, tk), :]
        fc = feat_ref[pl.ds(k0, tk), :]
        vf = jnp.concatenate([v_ref[pl.ds(k0, tk), :], fc], axis=1)
        if diag:
            s = lax.dot_general(qq, kc, _NT_DIMS, preferred_element_type=F32)
            kpos = k0 + lax.broadcasted_iota(jnp.int32, (1, tk), 1)
            s = s - slope * jnp.abs(qpos - kpos).astype(F32)
        else:
            lhs = jnp.where(j < jd, lhs_left, lhs_right)
            s = lax.dot_general(lhs, jnp.concatenate([kc, fc], axis=1), _NT_DIMS,
                                preferred_element_type=F32)
        m_new = jnp.maximum(m, jnp.max(s, axis=-1, keepdims=True))
        alpha = jnp.exp2(m - m_new)
        p = jnp.exp2(s - m_new).astype(BF16)
        acc = alpha * acc + jnp.dot(p, vf, preferred_element_type=F32)
        return m_new, acc

    carry = (jnp.full((2 * tq, 1), NEG_INF, F32), jnp.zeros((2 * tq, 2 * LANES), F32))
    carry = chunk(0, carry, True)
    for t in range(1, nchunks):
        carry = chunk(t, carry, False)
    _, acc = carry

    o = acc[:, :B_V_DIM] / acc[:, ONES_COL:ONES_COL + 1]
    lam = (jnp.exp(jnp.sum(lq1_ref[...] * lk1_ref[...], axis=-1, keepdims=True))
           - jnp.exp(jnp.sum(lq2_ref[...] * lk2_ref[...], axis=-1, keepdims=True)) + lam_init)
    diff = o[:tq] - lam * o[tq:]
    ms = jnp.mean(diff * diff, axis=-1, keepdims=True)
    y = diff * lax.rsqrt(ms + RMS_EPS) * gs_ref[...]
    o_ref[...] = (y * (1.0 - lam_init)).astype(BF16)


def _diff_attention(mix, feat, slopes_b, lq1, lk1, lq2, lk2, g_subln, layer, batch, seq_len,
                    tq=512, tk=512):
    n = mix.shape[0]
    tq = min(tq, seq_len)
    tk = min(tk, seq_len)
    nq = seq_len // tq
    qcol, kcol, vcol = (c // B_V_DIM for c in (MG_B_Q, MG_B_K, MG_B_V))
    lam_init = 0.8 - 0.6 * math.exp(-0.3 * layer)
    kern = functools.partial(_battn_kernel, tq=tq, tk=tk, seq_len=seq_len, lam_init=lam_init)
    vec = lambda width: pl.BlockSpec((None, 1, width), lambda b, h, i: (layer, 0, 0))
    return pl.pallas_call(
        kern,
        out_shape=jax.ShapeDtypeStruct((n, B_OUT), BF16),
        grid=(batch, B_HEADS, nq),
        in_specs=[
            pl.BlockSpec(memory_space=pltpu.SMEM),
            pl.BlockSpec((tq, B_V_DIM), lambda b, h, i: (b * nq + i, qcol + h)),
            pl.BlockSpec((None, tq, LANES), lambda b, h, i: (h, i, 0)),
            pl.BlockSpec((seq_len, B_V_DIM), lambda b, h, i: (b, kcol + h)),
            pl.BlockSpec((seq_len, B_V_DIM), lambda b, h, i: (b, vcol + h)),
            pl.BlockSpec((seq_len, LANES), lambda b, h, i: (0, 0)),
            vec(B_QK_DIM), vec(B_QK_DIM), vec(B_QK_DIM), vec(B_QK_DIM), vec(B_V_DIM),
        ],
        out_specs=pl.BlockSpec((tq, B_V_DIM), lambda b, h, i: (b * nq + i, h)),
        compiler_params=_params(("arbitrary", "arbitrary", "arbitrary"), 48),
        name="diff_attention",
    )(slopes_b, mix, _alibi_columns(slopes_b, seq_len), mix, mix, feat, lq1, lk1, lq2, lk2, g_subln)


def _band_window(qi, tq, half, length):
    kw = tq + 2 * half
    ks = jnp.clip(qi * tq - half, 0, length - kw)
    ks = pl.multiple_of(ks, half)
    qpos = qi * tq + lax.broadcasted_iota(jnp.int32, (tq, 1), 0)
    kpos = ks + lax.broadcasted_iota(jnp.int32, (1, kw), 1)
    rel = jnp.abs(qpos - kpos)
    return ks, kw, rel <= half, rel.astype(F32)


def _cattn_kernel(slope_ref, sink_ref, q_ref, k_ref, v_ref, o_ref, *, tq, seq_len):
    g = pl.program_id(1)
    qi = pl.program_id(2)
    ks, kw, valid, relf = _band_window(qi, tq, C_HALF_WINDOW, seq_len)
    kwin = k_ref[pl.ds(ks, kw), :]
    vwin = v_ref[pl.ds(ks, kw), :]
    for hh in range(C_GROUP):
        head = g * C_GROUP + hh
        slope = slope_ref[head] * LOG2E
        sink = sink_ref[head] * LOG2E
        cols = slice(hh * C_HEAD_DIM, (hh + 1) * C_HEAD_DIM)
        s = lax.dot_general(q_ref[:, cols], kwin, _NT_DIMS, preferred_element_type=F32)
        s = jnp.where(valid, s - slope * relf, NEG_INF)
        m = jnp.maximum(jnp.max(s, axis=-1, keepdims=True), sink)
        p = jnp.exp2(s - m)
        den = jnp.sum(p, axis=-1, keepdims=True) + jnp.exp2(sink - m)
        o = jnp.dot(p.astype(BF16), vwin, preferred_element_type=F32) / den
        o_ref[:, cols] = o.astype(BF16)


def _window_attention(mix, slopes_c, sink, batch, seq_len, tq=256):
    n = mix.shape[0]
    nq = seq_len // tq
    gw = C_GROUP * C_HEAD_DIM
    qcol = MG_C_Q // gw
    kcol = MG_C_K // C_HEAD_DIM
    vcol = MG_C_V // C_HEAD_DIM
    kern = functools.partial(_cattn_kernel, tq=tq, seq_len=seq_len)
    return pl.pallas_call(
        kern,
        out_shape=jax.ShapeDtypeStruct((n, C_OUT), BF16),
        grid=(batch, C_KV_HEADS, nq),
        in_specs=[
            pl.BlockSpec(memory_space=pltpu.SMEM),
            pl.BlockSpec(memory_space=pltpu.SMEM),
            pl.BlockSpec((tq, gw), lambda b, g, i: (b * nq + i, qcol + g)),
            pl.BlockSpec((seq_len, C_HEAD_DIM), lambda b, g, i: (b, kcol + g)),
            pl.BlockSpec((seq_len, C_HEAD_DIM), lambda b, g, i: (b, vcol + g)),
        ],
        out_specs=pl.BlockSpec((tq, gw), lambda b, g, i: (b * nq + i, g)),
        compiler_params=_params(("arbitrary", "arbitrary", "arbitrary"), 32),
        name="window_attention",
    )(slopes_c, sink, mix, mix, mix)


def _aattn_kernel(q_ref, k_ref, v_ref, o_ref, lse_ref, *scratch, tq, length, slopes, dil):
    qi = pl.program_id(1)
    nslab = A_GROUP_COLS // LANES
    half = A_GROUPS[0][0] // 2
    ks, kw, valid, relf = _band_window(qi, tq, half, length)
    lane = lax.broadcasted_iota(jnp.int32, (tq, A_GROUP_COLS), 1)

    def residue(r, _):
        kwin = k_ref[r, pl.ds(ks, kw), :]
        vwin = v_ref[r, pl.ds(ks, kw), :]
        q = q_ref[r]
        zero = jnp.zeros_like(q)
        out = jnp.zeros((tq, A_GROUP_COLS), F32)
        lse_out = jnp.zeros((tq, A_GROUP_COLS), F32)
        for hh in range(A_HEADS_PER_GROUP):
            in_head = (lane >= hh * A_HEAD_DIM) & (lane < (hh + 1) * A_HEAD_DIM)
            s = lax.dot_general(jnp.where(in_head, q, zero), kwin, _NT_DIMS,
                                preferred_element_type=F32)
            s = jnp.where(valid, s - (slopes[hh] * dil * LOG2E) * relf, NEG_INF)
            m = jnp.max(s, axis=-1, keepdims=True)
            p = jnp.exp2(s - m)
            den = jnp.sum(p, axis=-1, keepdims=True)
            o = jnp.dot(p.astype(BF16), vwin, preferred_element_type=F32) / den
            out = jnp.where(in_head, o, out)
            lse_out = jnp.where(in_head, m + jnp.log2(den), lse_out)
        if dil == 1:
            o_ref[...] = out
            lse_ref[...] = lse_out
        else:
            o_scr, lse_scr = scratch
            for c in range(nslab):
                cols = slice(c * LANES, (c + 1) * LANES)
                o_scr[c, pl.ds(r, tq, stride=dil), :] = out[:, cols]
                lse_scr[c, pl.ds(r, tq, stride=dil), :] = lse_out[:, cols]
        return 0

    if dil == 1:
        residue(0, 0)
    else:
        lax.fori_loop(0, dil, residue, 0)
        o_scr, lse_scr = scratch
        o_ref[...] = jnp.concatenate([o_scr[c] for c in range(nslab)], axis=1)
        lse_ref[...] = jnp.concatenate([lse_scr[c] for c in range(nslab)], axis=1)


def _dilated_attention(qkv, group, slopes, batch, seq_len, tq=256):
    window, dil = A_GROUPS[group]
    half = window // (2 * dil)
    length = seq_len // dil
    while tq > length - 2 * half or length % tq:
        tq //= 2
    nq = length // tq
    n = batch * seq_len
    kern = functools.partial(_aattn_kernel, tq=tq, length=length,
                             slopes=tuple(float(v) for v in slopes), dil=dil)
    out_sds = jax.ShapeDtypeStruct((n, A_GROUP_COLS), F32)
    out_spec = pl.BlockSpec((tq * dil, A_GROUP_COLS), lambda b, i: (b * nq + i, 0))
    kv_spec = lambda part: pl.BlockSpec((None, dil, length, A_GROUP_COLS), lambda b, i: (b, 0, 0, part),
                                        pipeline_mode=pl.Buffered(1))
    return pl.pallas_call(
        kern,
        out_shape=(out_sds, out_sds),
        grid=(batch, nq),
        in_specs=[
            pl.BlockSpec((None, dil, tq, A_GROUP_COLS), lambda b, i: (b, 0, i, 0)),
            kv_spec(1), kv_spec(2),
        ],
        out_specs=(out_spec, out_spec),
        scratch_shapes=[] if dil == 1 else
        [pltpu.VMEM((A_GROUP_COLS // LANES, tq * dil, LANES), F32)] * 2,
        compiler_params=_params(("arbitrary", "arbitrary"), 48),
        name=f"dilated_attention_g{group}",
    )(qkv, qkv, qkv)


def _merge_kernel(xa_ref, xb_ref, gt_ref, o0_ref, o1_ref, o2_ref, l0_ref, l1_ref, l2_ref, ob_ref, oc_ref,
                  ga_ref, gb_ref, gc_ref, wa_ref, wb_ref, wc_ref, wo_ref, g2_ref, sc2_ref, sh2_ref,
                  wr_ref, out_ref, h_ref, aff_ref, *, split):
    l0, l1, l2 = l0_ref[...], l1_ref[...], l2_ref[...]
    m = jnp.maximum(jnp.maximum(l0, l1), l2)
    e0, e1, e2 = jnp.exp2(l0 - m), jnp.exp2(l1 - m), jnp.exp2(l2 - m)
    oa = (e0 * o0_ref[...] + e1 * o1_ref[...] + e2 * o2_ref[...]) / (e0 + e1 + e2)
    ya = jnp.dot(oa.astype(BF16), wa_ref[...], preferred_element_type=F32)
    yb = jnp.dot(ob_ref[...], wb_ref[...], preferred_element_type=F32)
    yc = jnp.dot(oc_ref[...], wc_ref[...], preferred_element_type=F32)
    merged = (ga_ref[...].astype(F32) * ya + gb_ref[...].astype(F32) * yb
              + gc_ref[...].astype(F32) * yc)
    y = jnp.dot(merged.astype(BF16), wo_ref[...], preferred_element_type=F32)
    x = _two_part_load(xa_ref, xb_ref, split) + gt_ref[...] * y
    out_ref[...] = x

    h = _rms_mod(x, g2_ref[...], sc2_ref[...], sh2_ref[...])
    h_ref[...] = h.astype(BF16)
    w = wr_ref[...]
    h_hi = h.astype(BF16)
    h_lo = (h - h_hi.astype(F32)).astype(BF16)
    w_hi = w.astype(BF16)
    w_lo = (w - w_hi.astype(F32)).astype(BF16)
    logits = (jnp.dot(h_hi, w_hi, preferred_element_type=F32)
              + jnp.dot(h_lo, w_hi, preferred_element_type=F32)
              + jnp.dot(h_hi, w_lo, preferred_element_type=F32))
    mx = jnp.max(logits, axis=-1, keepdims=True)
    e = jnp.exp(logits - mx)
    aff_ref[...] = e / jnp.sum(e, axis=-1, keepdims=True)


def _merge(xa, xb, mod, oa_parts, ob, oc, mg, w_br_a, w_br_b, w_br_c, w_o, g2, w_router, layer,
           seq_len, tm=256):
    d = xa.shape[1]
    n = mg.shape[0]
    tm = min(tm, seq_len)
    split = min(xa.shape[0], n) // tm
    tiles_per_seq = seq_len // tm
    seq_of = lambda i: i // tiles_per_seq
    ne = w_router.shape[-1]
    (o0, l0), (o1, l1), (o2, l2) = oa_parts
    row = lambda width: pl.BlockSpec((tm, width), lambda i: (i, 0))
    resident = lambda rows: pl.BlockSpec((None, rows, d), lambda i: (layer, 0, 0),
                                         pipeline_mode=pl.Buffered(1))
    gate = lambda k: pl.BlockSpec((tm, d), lambda i: (i, k))
    modrow = lambda chunk: pl.BlockSpec((None, None, None, 1, d), _mod_spec(layer, chunk, seq_of))
    return pl.pallas_call(
        functools.partial(_merge_kernel, split=split),
        out_shape=(jax.ShapeDtypeStruct((n, d), F32), jax.ShapeDtypeStruct((n, d), BF16),
                   jax.ShapeDtypeStruct((n, ne), F32)),
        grid=(n // tm,),
        in_specs=[
            *_two_part_specs(tm, d, split), modrow(2),
            row(A_OUT), row(A_OUT), row(A_OUT), row(A_OUT), row(A_OUT), row(A_OUT),
            row(B_OUT), row(C_OUT),
            gate(0), gate(1), gate(2),
            resident(A_OUT), resident(B_OUT), resident(C_OUT), resident(d),
            pl.BlockSpec((None, 1, d), lambda i: (layer, 0, 0)), modrow(4), modrow(3),
            pl.BlockSpec((None, d, ne), lambda i: (layer, 0, 0)),
        ],
        out_specs=(row(d), row(d), row(ne)),
        compiler_params=_params(("arbitrary",), 56),
        name="merge",
    )(xa, xb, mod, o0, o1, o2, l0, l1, l2, ob, oc, mg, mg, mg, w_br_a, w_br_b, w_br_c, w_o,
      g2, mod, mod, w_router)


def _ffn_kernel(x_ref, gate_ref, wg_ref, wu_ref, wd_ref, o_ref, acc_scr):
    f = pl.program_id(2)

    @pl.when(f == 0)
    def _():
        acc_scr[...] = jnp.zeros_like(acc_scr)

    x = x_ref[...]
    a = jnp.dot(x, wg_ref[...], preferred_element_type=F32)
    u = jnp.dot(x, wu_ref[...], preferred_element_type=F32)
    hid = (a * jax.nn.sigmoid(a) * u).astype(BF16)
    acc_scr[...] += jnp.dot(hid, wd_ref[...], preferred_element_type=F32)

    @pl.when(f == pl.num_programs(2) - 1)
    def _():
        o_ref[...] = (acc_scr[...] * gate_ref[...]).astype(BF16)


def _expert_ffn(xe, gate, w_gate, w_up, w_down, layer, tm=1024, tf=512):
    ne, cap, d = xe.shape
    dff = w_gate.shape[-1]
    tm = min(tm, cap)
    return pl.pallas_call(
        _ffn_kernel,
        out_shape=jax.ShapeDtypeStruct((ne, cap, d), BF16),
        grid=(ne, cap // tm, dff // tf),
        in_specs=[
            pl.BlockSpec((None, tm, d), lambda e, i, f: (e, i, 0)),
            pl.BlockSpec((None, tm, 1), lambda e, i, f: (e, i, 0)),
            pl.BlockSpec((None, None, d, tf), lambda e, i, f: (layer, e, 0, f)),
            pl.BlockSpec((None, None, d, tf), lambda e, i, f: (layer, e, 0, f)),
            pl.BlockSpec((None, None, tf, d), lambda e, i, f: (layer, e, f, 0)),
        ],
        out_specs=pl.BlockSpec((None, tm, d), lambda e, i, f: (e, i, 0)),
        scratch_shapes=[pltpu.VMEM((tm, d), F32)],
        compiler_params=_params(("arbitrary", "arbitrary", "arbitrary"), 48),
        name="expert_ffn",
    )(xe, gate, w_gate, w_up, w_down)


WINDOW = 128
PAIR = 2 * WINDOW
N_PAIR_BUFS = 6


def _combine_copy(ye_hbm, buf, sem, expert, window, slot, half):
    start = pl.multiple_of(window * WINDOW, WINDOW)
    return pltpu.make_async_copy(ye_hbm.at[expert, pl.ds(start, WINDOW), :],
                                 buf.at[slot, pl.ds(half * WINDOW, WINDOW), :], sem.at[slot, half])


def _combine_kernel(cnt_ref, we_ref, ww_ref, idx_ref, x_ref, gt_ref, g_ref, sc_ref, sh_ref, ye_hbm,
                    o_ref, o2_ref, buf, sem, acc_scr, *, tm, final_norm, split):
    tile = pl.program_id(0)
    n = cnt_ref[tile]
    npairs = (n + 1) // 2
    tokens = tile * tm + lax.broadcasted_iota(jnp.int32, (tm, WINDOW), 0)
    acc_scr[...] = jnp.zeros_like(acc_scr)

    @pl.when(tile == 0)
    def _():
        buf[...] = jnp.zeros_like(buf)

    def copy(q, half):
        p = 2 * q + half
        return _combine_copy(ye_hbm, buf, sem, we_ref[tile, p], ww_ref[tile, p],
                             lax.rem(q, N_PAIR_BUFS), half)

    def start_pair(q):
        for half in range(2):
            @pl.when(2 * q + half < n)
            def _(half=half):
                copy(q, half).start()

    for ahead in range(N_PAIR_BUFS - 1):
        @pl.when(ahead < npairs)
        def _(ahead=ahead):
            start_pair(ahead)

    def body(q, _):
        for half in range(2):
            @pl.when(2 * q + half < n)
            def _(half=half):
                copy(q, half).wait()

        @pl.when(q + N_PAIR_BUFS - 1 < npairs)
        def _():
            start_pair(q + N_PAIR_BUFS - 1)

        blocks = []
        for half in range(2):
            p = 2 * q + half
            ids = idx_ref[we_ref[tile, p], ww_ref[tile, p]]
            blocks.append(jnp.where((ids == tokens) & (p < n), 1.0, 0.0).astype(BF16))
        onehot = jnp.concatenate(blocks, axis=1)
        acc_scr[...] += jnp.dot(onehot, buf[lax.rem(q, N_PAIR_BUFS)], preferred_element_type=F32)
        return 0

    lax.fori_loop(0, npairs, body, 0)
    x = x_ref[...] + gt_ref[...] * acc_scr[...]
    if final_norm:
        ms = jnp.mean(x * x, axis=-1, keepdims=True)
        y = x * lax.rsqrt(ms + RMS_EPS) * g_ref[...]

        @pl.when(tile < split)
        def _():
            o_ref[...] = y

        @pl.when(tile >= split)
        def _():
            o2_ref[...] = y
    else:
        o_ref[...] = x
        o2_ref[...] = _rms_mod(x, g_ref[...], sc_ref[...], sh_ref[...]).astype(BF16)


def _combine_plan(idx_sorted, n_tokens, tm):
    ne, cap = idx_sorted.shape
    nt = n_tokens // tm
    bounds = jnp.arange(nt + 1, dtype=jnp.int32) * tm
    base = jnp.sum(idx_sorted[:, :, None] < bounds[None, None, :], axis=1, dtype=jnp.int32)
    lo, hi = base[:, :-1], base[:, 1:]
    first = lo // WINDOW
    nwin = jnp.where(hi > lo, (hi - 1) // WINDOW - first + 1, 0)
    cum_incl = jnp.cumsum(nwin, axis=0)
    cum_excl = cum_incl - nwin
    count = cum_incl[-1]
    max_pairs = ne * (tm // WINDOW + 1) + 1
    p = jnp.arange(max_pairs, dtype=jnp.int32)
    expert = jnp.sum(cum_incl.T[:, None, :] <= p[None, :, None], axis=-1, dtype=jnp.int32)
    expert = jnp.minimum(expert, ne - 1)
    k = p[None, :] - jnp.take_along_axis(cum_excl.T, expert, axis=1)
    window = jnp.take_along_axis(first.T, expert, axis=1) + k
    window = jnp.clip(window, 0, cap // WINDOW - 1)
    return count.astype(jnp.int32), expert, window.astype(jnp.int32)


def _combine(x, ye, idx_sorted, mod, g_next, layer, seq_len, final_norm, n_first=0, tm=512):
    n, d = x.shape
    tm = min(tm, seq_len)
    tiles_per_seq = seq_len // tm
    seq_of = lambda i, *_: i // tiles_per_seq
    count, expert, window = _combine_plan(idx_sorted, n, tm)
    ne, cap = idx_sorted.shape
    idx_windows = idx_sorted.reshape(ne, cap // WINDOW, 1, WINDOW)
    split = n_first // tm
    kern = functools.partial(_combine_kernel, tm=tm, final_norm=final_norm, split=split)
    row = pl.BlockSpec((tm, d), lambda i, *_: (i, 0))
    modrow = lambda lyr, chunk: pl.BlockSpec((None, None, None, 1, d), _mod_spec(lyr, chunk, seq_of))
    if final_norm:
        g_spec = pl.BlockSpec((1, d), lambda i, *_: (0, 0))
        norm_specs = [g_spec, modrow(layer, 1), modrow(layer, 0)]
        out_shape = (jax.ShapeDtypeStruct((n_first, d), F32),
                     jax.ShapeDtypeStruct((n - n_first, d), F32))
        out_specs = _two_part_specs(tm, d, split)
    else:
        g_spec = pl.BlockSpec((None, 1, d), lambda i, *_: (layer + 1, 0, 0))
        norm_specs = [g_spec, modrow(layer + 1, 1), modrow(layer + 1, 0)]
        out_shape = (jax.ShapeDtypeStruct((n, d), F32), jax.ShapeDtypeStruct((n, d), BF16))
        out_specs = (row, row)
    grid_spec = pltpu.PrefetchScalarGridSpec(
        num_scalar_prefetch=3,
        grid=(n // tm,),
        in_specs=[
            pl.BlockSpec(idx_windows.shape, lambda i, *_: (0, 0, 0, 0)),
            row,
            modrow(layer, N_MOD - 1),
            *norm_specs,
            pl.BlockSpec(memory_space=pl.ANY),
        ],
        out_specs=out_specs,
        scratch_shapes=[pltpu.VMEM((N_PAIR_BUFS, PAIR, d), BF16),
                        pltpu.SemaphoreType.DMA((N_PAIR_BUFS, 2)),
                        pltpu.VMEM((tm, d), F32)],
    )
    return pl.pallas_call(
        kern,
        out_shape=out_shape,
        grid_spec=grid_spec,
        compiler_params=_params(("arbitrary",), 40),
        name="combine",
    )(count, expert, window, idx_windows, x, mod, g_next, mod, mod, ye)


def _trunk(xa, xb, c_rows, seq_len, g_norm1, g_norm2, w_ada, b_ada, w_in, w_branch_gate,
           w_br_a, w_br_b, w_br_c, w_o, lambda_q1, lambda_k1, lambda_q2, lambda_k2, g_subln, sink,
           w_router, w_e_gate, w_e_up, w_e_down, g_final):
    d = xa.shape[1]
    group_sizes = (xa.shape[0], xb.shape[0])
    n = sum(group_sizes)
    depth = w_in.shape[0]
    batch = n // seq_len
    s_a, s_b, s_c = _alibi_slopes()

    w_cat, col_scale = _proj_weights(w_in, w_branch_gate)
    wa, wb, wc, wo = (w.astype(BF16) for w in (w_br_a, w_br_b, w_br_c, w_o))
    weg, weu, wed = (w.astype(BF16) for w in (w_e_gate, w_e_up, w_e_down))
    row3 = lambda a: a.reshape(depth, 1, a.shape[-1])
    g1, g2 = row3(g_norm1), row3(g_norm2)
    lq1, lk1, lq2, lk2, gsub = (row3(a) for a in (lambda_q1, lambda_k1, lambda_q2, lambda_k2, g_subln))
    feat = _position_features(seq_len)

    mod = _ada_modulation(c_rows, w_ada, b_ada)
    mod = mod.reshape(depth, c_rows.shape[0], N_MOD, 1, d)

    h1 = _norm_mod(xa, xb, g1, mod, 0, seq_len)
    for layer in range(depth):
        *a_qkv, mg = _proj(h1, w_cat, col_scale, layer, batch, seq_len)
        oa_parts = [
            _dilated_attention(a_qkv[gi], gi,
                               s_a[gi * A_HEADS_PER_GROUP:(gi + 1) * A_HEADS_PER_GROUP],
                               batch, seq_len)
            for gi in range(len(A_GROUPS))
        ]
        ob = _diff_attention(mg, feat, jnp.asarray(s_b), lq1, lk1, lq2, lk2, gsub, layer, batch,
                             seq_len)
        oc = _window_attention(mg, jnp.asarray(s_c), sink[layer], batch, seq_len)
        x, h2, aff = _merge(xa, xb, mod, oa_parts, ob, oc, mg, wa, wb, wc, wo, g2, w_router, layer,
                            seq_len)
        aff_t = aff.T

        idx_parts, gate_parts = [], []
        start = 0
        for size in group_sizes:
            cap = (EC_CAPACITY_FACTOR * size) // N_EXPERTS
            gate, idx = lax.top_k(aff_t[:, start:start + size], cap)
            idx_parts.append(idx + start)
            gate_parts.append(gate)
            start += size
        idx = jnp.concatenate(idx_parts, axis=1)
        gate = jnp.concatenate(gate_parts, axis=1)
        idx, gate = lax.sort((idx, gate), dimension=1, num_keys=1)
        xe = jnp.take(h2, idx, axis=0, mode="clip")
        ye = _expert_ffn(xe, gate[..., None], weg, weu, wed, layer)
        if layer == depth - 1:
            return _combine(x, ye, idx, mod, g_final.reshape(1, d), layer, seq_len, True,
                            n_first=group_sizes[0])
        xa, h1 = _combine(x, ye, idx, mod, g1, layer, seq_len, False)
        xb = xa


def kernel(x_prompt, x_sample, c_prompt, c_sample, g_norm1, g_norm2, w_ada, b_ada, w_in, w_branch_gate, w_br_a, w_br_b, w_br_c, w_o, lambda_q1, lambda_k1, lambda_q2, lambda_k2, g_subln, sink, w_router, w_e_gate, w_e_up, w_e_down, g_final):
    bp, seq_len, d = x_prompt.shape
    bs = x_sample.shape[0]
    assert x_sample.shape[1] == seq_len
    c = jnp.concatenate([c_prompt, c_sample], axis=0)
    pad = -c.shape[0] % SUBLANES
    c_rows = jnp.pad(c, ((0, pad), (0, 0)))
    y_prompt, y_sample = _trunk(
        x_prompt.reshape(bp * seq_len, d), x_sample.reshape(bs * seq_len, d), c_rows, seq_len,
        g_norm1, g_norm2, w_ada, b_ada, w_in, w_branch_gate, w_br_a, w_br_b, w_br_c, w_o,
        lambda_q1, lambda_k1, lambda_q2, lambda_k2, g_subln, sink, w_router, w_e_gate, w_e_up,
        w_e_down, g_final)
    return (y_prompt.reshape(bp, seq_len, d), y_sample.reshape(bs, seq_len, d))
```

```python
import functools
import math

import numpy as np
import jax
import jax.numpy as jnp
from jax import lax
from jax.experimental import pallas as pl
from jax.experimental.pallas import tpu as pltpu

F32 = jnp.float32
BF16 = jnp.bfloat16

A_GROUPS = ((128, 1), (512, 4), (2048, 16))
A_HEADS_PER_GROUP = 4
A_HEAD_DIM = 64
A_N_HEADS = 12
A_OUT = 256
A_GROUP_COLS = A_HEADS_PER_GROUP * A_HEAD_DIM
A_TILE = 3 * A_GROUP_COLS
B_HEADS = 6
B_QK_DIM = 64
B_V_DIM = 128
B_OUT = 768
C_Q_HEADS = 8
C_KV_HEADS = 2
C_GROUP = C_Q_HEADS // C_KV_HEADS
C_HEAD_DIM = 128
C_HALF_WINDOW = 128
C_OUT = 1024
A_COLS = 2304
B_COLS = 2304
C_COLS = 1536
IN_COLS = 6144
N_BRANCH = 3
N_EXPERTS = 16
EC_CAPACITY_FACTOR = 2
N_ALIBI_HEADS = 26
RMS_EPS = 1e-6
NEG_INF = -1e30
LOG2E = math.log2(math.e)
N_MOD = 6
SUBLANES = 8
LANES = 128
POS_SPLIT = 64
N_PIECES = 3

_NT_DIMS = (((1,), (1,)), ((), ()))
_MIB = 1024 * 1024


def _alibi_slopes():
    n = N_ALIBI_HEADS
    s = 2.0 ** (-8.0 * np.arange(1, n + 1, dtype=np.float32) / n)
    s = s.astype(np.float32)
    s_c = s[:C_Q_HEADS]
    s_a = s[C_Q_HEADS:C_Q_HEADS + A_N_HEADS]
    s_b = s[C_Q_HEADS + A_N_HEADS:]
    return s_a, s_b, s_c


def _params(semantics, vmem_mib):
    return pltpu.CompilerParams(dimension_semantics=semantics,
                                vmem_limit_bytes=vmem_mib * _MIB)


def _mod_spec(layer, chunk, seq_of):
    def index(*ids):
        return (layer, seq_of(*ids), chunk, 0, 0)
    return index


def _rms_mod(x, g, scale, shift):
    ms = jnp.mean(x * x, axis=-1, keepdims=True)
    y = x * lax.rsqrt(ms + RMS_EPS) * g
    return y * (1.0 + scale) + shift


def _split3(x):
    hi = x.astype(BF16).astype(F32)
    mid = (x - hi).astype(BF16).astype(F32)
    lo = (x - hi - mid).astype(BF16).astype(F32)
    return hi, mid, lo


def _ada_kernel(c_ref, w_ref, b_ref, o_ref):
    c = c_ref[...]
    a = c * jax.nn.sigmoid(c)
    o_ref[...] = jnp.dot(a, w_ref[...], precision=lax.Precision.HIGHEST,
                         preferred_element_type=F32) + b_ref[...]


def _ada_modulation(c_rows, w_ada, b_ada, tn=1024):
    depth, d, n6 = w_ada.shape
    rows = c_rows.shape[0]
    return pl.pallas_call(
        _ada_kernel,
        out_shape=jax.ShapeDtypeStruct((depth, rows, n6), F32),
        grid=(depth, n6 // tn),
        in_specs=[
            pl.BlockSpec((rows, d), lambda l, j: (0, 0)),
            pl.BlockSpec((None, d, tn), lambda l, j: (l, 0, j)),
            pl.BlockSpec((None, 1, tn), lambda l, j: (l, 0, j)),
        ],
        out_specs=pl.BlockSpec((None, rows, tn), lambda l, j: (l, 0, j)),
        compiler_params=_params(("arbitrary", "arbitrary"), 40),
        name="ada_modulation",
    )(c_rows, w_ada, b_ada.reshape(depth, 1, n6))


def _two_part_specs(tm, d, split):
    first = pl.BlockSpec((tm, d), lambda i, *_: (jnp.minimum(i, split - 1), 0))
    second = pl.BlockSpec((tm, d), lambda i, *_: (jnp.maximum(i - split, 0), 0))
    return first, second


def _two_part_load(xa_ref, xb_ref, split):
    return jnp.where(pl.program_id(0) < split, xa_ref[...], xb_ref[...])


def _norm_kernel(xa_ref, xb_ref, g_ref, sc_ref, sh_ref, h_ref, *, split):
    x = _two_part_load(xa_ref, xb_ref, split)
    h_ref[...] = _rms_mod(x, g_ref[...], sc_ref[...], sh_ref[...]).astype(BF16)


def _norm_mod(xa, xb, g, mod, layer, seq_len, tm=512):
    d = xa.shape[1]
    n = xa.shape[0] + xb.shape[0]
    tm = min(tm, seq_len)
    tps = seq_len // tm
    seq_of = lambda i: i // tps
    split = xa.shape[0] // tm
    return pl.pallas_call(
        functools.partial(_norm_kernel, split=split),
        out_shape=jax.ShapeDtypeStruct((n, d), BF16),
        grid=(n // tm,),
        in_specs=[*_two_part_specs(tm, d, split),
                  pl.BlockSpec((None, 1, d), lambda i: (layer, 0, 0)),
                  pl.BlockSpec((None, None, None, 1, d), _mod_spec(layer, 1, seq_of)),
                  pl.BlockSpec((None, None, None, 1, d), _mod_spec(layer, 0, seq_of))],
        out_specs=pl.BlockSpec((tm, d), lambda i: (i, 0)),
        compiler_params=_params(("arbitrary",), 32),
        name="norm_mod",
    )(xa, xb, g, mod, mod)


PROJ_TILE = A_TILE
N_A_TILES = len(A_GROUPS)
N_GATE_TILES = IN_COLS // PROJ_TILE
N_MIX_TILES = (C_COLS + B_COLS) // PROJ_TILE
PROJ_ROW_SPLIT = 2
MG_C_Q = IN_COLS
MG_C_K = MG_C_Q + C_OUT
MG_C_V = MG_C_K + C_KV_HEADS * C_HEAD_DIM
MG_B_Q = MG_C_Q + C_COLS
MG_B_K = MG_B_Q + B_OUT
MG_B_V = MG_B_K + B_OUT


def _proj_kernel(h_ref, w_ref, cs_ref, a0_ref, a1_ref, a2_ref, mg_ref, acc_scr, *, tm):
    j = pl.program_id(1)
    nslab = PROJ_TILE // LANES

    @pl.when(j < N_A_TILES)
    def _():
        acc = jnp.dot(h_ref[...], w_ref[...], preferred_element_type=F32) * cs_ref[...]

        @pl.when(j == 0)
        def _():
            a0_ref[0] = acc.astype(BF16)

        @pl.when(j > 0)
        def _():
            for c in range(nslab):
                acc_scr[c] = acc[:, c * LANES:(c + 1) * LANES]

        for gi, a_ref in ((1, a1_ref), (2, a2_ref)):
            dil = A_GROUPS[gi][1]

            @pl.when(j == gi)
            def _(a_ref=a_ref, dil=dil):
                for r in range(dil):
                    rows = [acc_scr[c, pl.ds(r, tm // dil, stride=dil), :] for c in range(nslab)]
                    a_ref[r] = jnp.concatenate(rows, axis=1).astype(BF16)

    @pl.when(j >= N_A_TILES)
    def _():
        is_gate = j < N_A_TILES + N_GATE_TILES
        for part in range(PROJ_ROW_SPLIT):
            rows = slice(part * tm // PROJ_ROW_SPLIT, (part + 1) * tm // PROJ_ROW_SPLIT)
            acc = jnp.dot(h_ref[rows, :], w_ref[...], preferred_element_type=F32) * cs_ref[...]
            mg_ref[rows, :] = jnp.where(is_gate, 0.5 * jnp.tanh(0.5 * acc) + 0.5, acc).astype(BF16)


def _proj(h, w_cat, col_scale, layer, batch, seq_len, tm=1024):
    n, d = h.shape
    tn = PROJ_TILE
    tm = min(tm, seq_len)
    tps = seq_len // tm
    a_shapes, a_specs = [], []
    for _, dil in A_GROUPS:
        a_shapes.append(jax.ShapeDtypeStruct((batch, dil, seq_len // dil, tn), BF16))
        a_specs.append(pl.BlockSpec((None, dil, tm // dil, tn), lambda i, j: (i // tps, 0, i % tps, 0)))
    n_mg = N_GATE_TILES + N_MIX_TILES
    kern = functools.partial(_proj_kernel, tm=tm)
    return pl.pallas_call(
        kern,
        out_shape=(*a_shapes, jax.ShapeDtypeStruct((n, n_mg * tn), BF16)),
        grid=(n // tm, N_A_TILES + n_mg),
        in_specs=[
            pl.BlockSpec((tm, d), lambda i, j: (i, 0)),
            pl.BlockSpec((None, d, tn), lambda i, j: (layer, 0, j)),
            pl.BlockSpec((1, tn), lambda i, j: (0, j)),
        ],
        out_specs=(*a_specs,
                   pl.BlockSpec((tm, tn), lambda i, j: (i, jnp.clip(j - N_A_TILES, 0, n_mg - 1)))),
        scratch_shapes=[pltpu.VMEM((tn // LANES, tm, LANES), F32)],
        compiler_params=_params(("arbitrary", "arbitrary"), 48),
        name="proj",
    )(h, w_cat, col_scale)


def _proj_weights(w_in, w_branch_gate):
    cols = []
    for gi in range(len(A_GROUPS)):
        for part in range(3):
            start = part * (A_COLS // 3) + gi * A_GROUP_COLS
            cols.append(np.arange(start, start + A_GROUP_COLS))
    perm_a = np.concatenate(cols)
    c0 = A_COLS + B_COLS
    w_cat = jnp.concatenate([w_in[..., perm_a], w_branch_gate, w_in[..., c0:c0 + C_COLS],
                             w_in[..., A_COLS:A_COLS + B_COLS]], axis=-1).astype(BF16)

    scale = np.ones((2 * IN_COLS,), np.float32)
    for gi in range(len(A_GROUPS)):
        scale[gi * A_TILE:gi * A_TILE + A_GROUP_COLS] = A_HEAD_DIM ** -0.5 * LOG2E
    qc = A_COLS + MG_C_Q
    scale[qc:qc + C_OUT] = C_HEAD_DIM ** -0.5 * LOG2E
    qb = A_COLS + MG_B_Q
    scale[qb:qb + B_OUT] = B_QK_DIM ** -0.5 * LOG2E
    return w_cat, jnp.asarray(scale).reshape(1, -1)


def _position_features(seq_len):
    j = np.arange(seq_len)
    feat = np.zeros((seq_len, LANES), np.float32)
    feat[:, 0:N_PIECES] = ((j // POS_SPLIT) * POS_SPLIT)[:, None]
    feat[:, N_PIECES:2 * N_PIECES] = (j % POS_SPLIT)[:, None]
    feat[:, 2 * N_PIECES:3 * N_PIECES] = 1.0
    return jnp.asarray(feat, BF16)


ONES_COL = B_V_DIM + 2 * N_PIECES


def _alibi_columns(slopes, seq_len):
    slope = (slopes * LOG2E)[:, None]
    pos = jnp.arange(seq_len, dtype=F32)[None, :]
    s3 = [jnp.broadcast_to(-piece, (slopes.shape[0], seq_len)) for piece in _split3(slope)]
    c3 = list(_split3(slope * pos))
    cols = jnp.stack(s3 + s3 + c3, axis=-1)
    return jnp.pad(cols, ((0, 0), (0, 0), (0, LANES - cols.shape[-1]))).astype(BF16)


def _battn_kernel(slope_ref, q_ref, aug_ref, k_ref, v_ref, feat_ref, lq1_ref, lk1_ref, lq2_ref, lk2_ref,
                  gs_ref, o_ref, *, tq, tk, seq_len, lam_init):
    h = pl.program_id(1)
    qi = pl.program_id(2)
    slope = slope_ref[h] * LOG2E
    q = q_ref[...]
    lane = lax.broadcasted_iota(jnp.int32, (tq, LANES), 1)
    zero = jnp.zeros_like(q)
    qq = jnp.concatenate([jnp.where(lane < B_QK_DIM, q, zero),
                          jnp.where(lane >= B_QK_DIM, q, zero)], axis=0)
    row = lax.broadcasted_iota(jnp.int32, (2 * tq, 1), 0)
    qpos = qi * tq + jnp.where(row >= tq, row - tq, row)

    aug = jnp.concatenate([aug_ref[...], aug_ref[...]], axis=0)
    lhs_right = jnp.concatenate([qq, aug], axis=1)
    lhs_left = jnp.concatenate([qq, -aug], axis=1)

    nchunks = seq_len // tk
    jd = (qi * tq) // tk

    def chunk(t, carry, diag):
        m, acc = carry
        j = lax.rem(jd + t, nchunks)
        k0 = pl.multiple_of(j * tk, tk)
        kc = k_ref[pl.ds(k0, tk), :]
        fc = feat_ref[pl.ds(k0, tk), :]
        vf = jnp.concatenate([v_ref[pl.ds(k0, tk), :], fc], axis=1)
        if diag:
            s = lax.dot_general(qq, kc, _NT_DIMS, preferred_element_type=F32)
            kpos = k0 + lax.broadcasted_iota(jnp.int32, (1, tk), 1)
            s = s - slope * jnp.abs(qpos - kpos).astype(F32)
        else:
            lhs = jnp.where(j < jd, lhs_left, lhs_right)
            s = lax.dot_general(lhs, jnp.concatenate([kc, fc], axis=1), _NT_DIMS,
                                preferred_element_type=F32)
        m_new = jnp.maximum(m, jnp.max(s, axis=-1, keepdims=True))
        alpha = jnp.exp2(m - m_new)
        p = jnp.exp2(s - m_new).astype(BF16)
        acc = alpha * acc + jnp.dot(p, vf, preferred_element_type=F32)
        return m_new, acc

    carry = (jnp.full((2 * tq, 1), NEG_INF, F32), jnp.zeros((2 * tq, 2 * LANES), F32))
    carry = chunk(0, carry, True)
    for t in range(1, nchunks):
        carry = chunk(t, carry, False)
    _, acc = carry

    o = acc[:, :B_V_DIM] / acc[:, ONES_COL:ONES_COL + 1]
    lam = (jnp.exp(jnp.sum(lq1_ref[...] * lk1_ref[...], axis=-1, keepdims=True))
           - jnp.exp(jnp.sum(lq2_ref[...] * lk2_ref[...], axis=-1, keepdims=True)) + lam_init)
    diff = o[:tq] - lam * o[tq:]
    ms = jnp.mean(diff * diff, axis=-1, keepdims=True)
    y = diff * lax.rsqrt(ms + RMS_EPS) * gs_ref[...]
    o_ref[...] = (y * (1.0 - lam_init)).astype(BF16)


def _diff_attention(mix, feat, slopes_b, lq1, lk1, lq2, lk2, g_subln, layer, batch, seq_len,
                    tq=512, tk=512):
    n = mix.shape[0]
    tq = min(tq, seq_len)
    tk = min(tk, seq_len)
    nq = seq_len // tq
    qcol, kcol, vcol = (c // B_V_DIM for c in (MG_B_Q, MG_B_K, MG_B_V))
    lam_init = 0.8 - 0.6 * math.exp(-0.3 * layer)
    kern = functools.partial(_battn_kernel, tq=tq, tk=tk, seq_len=seq_len, lam_init=lam_init)
    vec = lambda width: pl.BlockSpec((None, 1, width), lambda b, h, i: (layer, 0, 0))
    return pl.pallas_call(
        kern,
        out_shape=jax.ShapeDtypeStruct((n, B_OUT), BF16),
        grid=(batch, B_HEADS, nq),
        in_specs=[
            pl.BlockSpec(memory_space=pltpu.SMEM),
            pl.BlockSpec((tq, B_V_DIM), lambda b, h, i: (b * nq + i, qcol + h)),
            pl.BlockSpec((None, tq, LANES), lambda b, h, i: (h, i, 0)),
            pl.BlockSpec((seq_len, B_V_DIM), lambda b, h, i: (b, kcol + h)),
            pl.BlockSpec((seq_len, B_V_DIM), lambda b, h, i: (b, vcol + h)),
            pl.BlockSpec((seq_len, LANES), lambda b, h, i: (0, 0)),
            vec(B_QK_DIM), vec(B_QK_DIM), vec(B_QK_DIM), vec(B_QK_DIM), vec(B_V_DIM),
        ],
        out_specs=pl.BlockSpec((tq, B_V_DIM), lambda b, h, i: (b * nq + i, h)),
        compiler_params=_params(("arbitrary", "arbitrary", "arbitrary"), 48),
        name="diff_attention",
    )(slopes_b, mix, _alibi_columns(slopes_b, seq_len), mix, mix, feat, lq1, lk1, lq2, lk2, g_subln)


def _band_window(qi, tq, half, length):
    kw = tq + 2 * half
    ks = jnp.clip(qi * tq - half, 0, length - kw)
    ks = pl.multiple_of(ks, half)
    qpos = qi * tq + lax.broadcasted_iota(jnp.int32, (tq, 1), 0)
    kpos = ks + lax.broadcasted_iota(jnp.int32, (1, kw), 1)
    rel = jnp.abs(qpos - kpos)
    return ks, kw, rel <= half, rel.astype(F32)


def _cattn_kernel(slope_ref, sink_ref, q_ref, k_ref, v_ref, o_ref, *, tq, seq_len):
    g = pl.program_id(1)
    qi = pl.program_id(2)
    ks, kw, valid, relf = _band_window(qi, tq, C_HALF_WINDOW, seq_len)
    kwin = k_ref[pl.ds(ks, kw), :]
    vwin = v_ref[pl.ds(ks, kw), :]
    for hh in range(C_GROUP):
        head = g * C_GROUP + hh
        slope = slope_ref[head] * LOG2E
        sink = sink_ref[head] * LOG2E
        cols = slice(hh * C_HEAD_DIM, (hh + 1) * C_HEAD_DIM)
        s = lax.dot_general(q_ref[:, cols], kwin, _NT_DIMS, preferred_element_type=F32)
        s = jnp.where(valid, s - slope * relf, NEG_INF)
        m = jnp.maximum(jnp.max(s, axis=-1, keepdims=True), sink)
        p = jnp.exp2(s - m)
        den = jnp.sum(p, axis=-1, keepdims=True) + jnp.exp2(sink - m)
        o = jnp.dot(p.astype(BF16), vwin, preferred_element_type=F32) / den
        o_ref[:, cols] = o.astype(BF16)


def _window_attention(mix, slopes_c, sink, batch, seq_len, tq=256):
    n = mix.shape[0]
    nq = seq_len // tq
    gw = C_GROUP * C_HEAD_DIM
    qcol = MG_C_Q // gw
    kcol = MG_C_K // C_HEAD_DIM
    vcol = MG_C_V // C_HEAD_DIM
    kern = functools.partial(_cattn_kernel, tq=tq, seq_len=seq_len)
    return pl.pallas_call(
        kern,
        out_shape=jax.ShapeDtypeStruct((n, C_OUT), BF16),
        grid=(batch, C_KV_HEADS, nq),
        in_specs=[
            pl.BlockSpec(memory_space=pltpu.SMEM),
            pl.BlockSpec(memory_space=pltpu.SMEM),
            pl.BlockSpec((tq, gw), lambda b, g, i: (b * nq + i, qcol + g)),
            pl.BlockSpec((seq_len, C_HEAD_DIM), lambda b, g, i: (b, kcol + g)),
            pl.BlockSpec((seq_len, C_HEAD_DIM), lambda b, g, i: (b, vcol + g)),
        ],
        out_specs=pl.BlockSpec((tq, gw), lambda b, g, i: (b * nq + i, g)),
        compiler_params=_params(("arbitrary", "arbitrary", "arbitrary"), 32),
        name="window_attention",
    )(slopes_c, sink, mix, mix, mix)


def _aattn_kernel(q_ref, k_ref, v_ref, o_ref, lse_ref, *scratch, tq, length, slopes, dil):
    qi = pl.program_id(1)
    nslab = A_GROUP_COLS // LANES
    half = A_GROUPS[0][0] // 2
    ks, kw, valid, relf = _band_window(qi, tq, half, length)
    lane = lax.broadcasted_iota(jnp.int32, (tq, A_GROUP_COLS), 1)

    def residue(r, _):
        kwin = k_ref[r, pl.ds(ks, kw), :]
        vwin = v_ref[r, pl.ds(ks, kw), :]
        q = q_ref[r]
        zero = jnp.zeros_like(q)
        out = jnp.zeros((tq, A_GROUP_COLS), F32)
        lse_out = jnp.zeros((tq, A_GROUP_COLS), F32)
        for hh in range(A_HEADS_PER_GROUP):
            in_head = (lane >= hh * A_HEAD_DIM) & (lane < (hh + 1) * A_HEAD_DIM)
            s = lax.dot_general(jnp.where(in_head, q, zero), kwin, _NT_DIMS,
                                preferred_element_type=F32)
            s = jnp.where(valid, s - (slopes[hh] * dil * LOG2E) * relf, NEG_INF)
            m = jnp.max(s, axis=-1, keepdims=True)
            p = jnp.exp2(s - m)
            den = jnp.sum(p, axis=-1, keepdims=True)
            o = jnp.dot(p.astype(BF16), vwin, preferred_element_type=F32) / den
            out = jnp.where(in_head, o, out)
            lse_out = jnp.where(in_head, m + jnp.log2(den), lse_out)
        if dil == 1:
            o_ref[...] = out
            lse_ref[...] = lse_out
        else:
            o_scr, lse_scr = scratch
            for c in range(nslab):
                cols = slice(c * LANES, (c + 1) * LANES)
                o_scr[c, pl.ds(r, tq, stride=dil), :] = out[:, cols]
                lse_scr[c, pl.ds(r, tq, stride=dil), :] = lse_out[:, cols]
        return 0

    if dil == 1:
        residue(0, 0)
    else:
        lax.fori_loop(0, dil, residue, 0)
        o_scr, lse_scr = scratch
        o_ref[...] = jnp.concatenate([o_scr[c] for c in range(nslab)], axis=1)
        lse_ref[...] = jnp.concatenate([lse_scr[c] for c in range(nslab)], axis=1)


def _dilated_attention(qkv, group, slopes, batch, seq_len, tq=256):
    window, dil = A_GROUPS[group]
    half = window // (2 * dil)
    length = seq_len // dil
    while tq > length - 2 * half or length % tq:
        tq //= 2
    nq = length // tq
    n = batch * seq_len
    kern = functools.partial(_aattn_kernel, tq=tq, length=length,
                             slopes=tuple(float(v) for v in slopes), dil=dil)
    out_sds = jax.ShapeDtypeStruct((n, A_GROUP_COLS), F32)
    out_spec = pl.BlockSpec((tq * dil, A_GROUP_COLS), lambda b, i: (b * nq + i, 0))
    kv_spec = lambda part: pl.BlockSpec((None, dil, length, A_GROUP_COLS), lambda b, i: (b, 0, 0, part),
                                        pipeline_mode=pl.Buffered(1))
    return pl.pallas_call(
        kern,
        out_shape=(out_sds, out_sds),
        grid=(batch, nq),
        in_specs=[
            pl.BlockSpec((None, dil, tq, A_GROUP_COLS), lambda b, i: (b, 0, i, 0)),
            kv_spec(1), kv_spec(2),
        ],
        out_specs=(out_spec, out_spec),
        scratch_shapes=[] if dil == 1 else
        [pltpu.VMEM((A_GROUP_COLS // LANES, tq * dil, LANES), F32)] * 2,
        compiler_params=_params(("arbitrary", "arbitrary"), 48),
        name=f"dilated_attention_g{group}",
    )(qkv, qkv, qkv)


def _merge_kernel(xa_ref, xb_ref, gt_ref, o0_ref, o1_ref, o2_ref, l0_ref, l1_ref, l2_ref, ob_ref, oc_ref,
                  ga_ref, gb_ref, gc_ref, wa_ref, wb_ref, wc_ref, wo_ref, g2_ref, sc2_ref, sh2_ref,
                  wr_ref, out_ref, h_ref, aff_ref, *, split):
    l0, l1, l2 = l0_ref[...], l1_ref[...], l2_ref[...]
    m = jnp.maximum(jnp.maximum(l0, l1), l2)
    e0, e1, e2 = jnp.exp2(l0 - m), jnp.exp2(l1 - m), jnp.exp2(l2 - m)
    oa = (e0 * o0_ref[...] + e1 * o1_ref[...] + e2 * o2_ref[...]) / (e0 + e1 + e2)
    ya = jnp.dot(oa.astype(BF16), wa_ref[...], preferred_element_type=F32)
    yb = jnp.dot(ob_ref[...], wb_ref[...], preferred_element_type=F32)
    yc = jnp.dot(oc_ref[...], wc_ref[...], preferred_element_type=F32)
    merged = (ga_ref[...].astype(F32) * ya + gb_ref[...].astype(F32) * yb
              + gc_ref[...].astype(F32) * yc)
    y = jnp.dot(merged.astype(BF16), wo_ref[...], preferred_element_type=F32)
    x = _two_part_load(xa_ref, xb_ref, split) + gt_ref[...] * y
    out_ref[...] = x

    h = _rms_mod(x, g2_ref[...], sc2_ref[...], sh2_ref[...])
    h_ref[...] = h.astype(BF16)
    w = wr_ref[...]
    h_hi = h.astype(BF16)
    h_lo = (h - h_hi.astype(F32)).astype(BF16)
    w_hi = w.astype(BF16)
    w_lo = (w - w_hi.astype(F32)).astype(BF16)
    logits = (jnp.dot(h_hi, w_hi, preferred_element_type=F32)
              + jnp.dot(h_lo, w_hi, preferred_element_type=F32)
              + jnp.dot(h_hi, w_lo, preferred_element_type=F32))
    mx = jnp.max(logits, axis=-1, keepdims=True)
    e = jnp.exp(logits - mx)
    aff_ref[...] = e / jnp.sum(e, axis=-1, keepdims=True)


def _merge(xa, xb, mod, oa_parts, ob, oc, mg, w_br_a, w_br_b, w_br_c, w_o, g2, w_router, layer,
           seq_len, tm=256):
    d = xa.shape[1]
    n = mg.shape[0]
    tm = min(tm, seq_len)
    split = min(xa.shape[0], n) // tm
    tiles_per_seq = seq_len // tm
    seq_of = lambda i: i // tiles_per_seq
    ne = w_router.shape[-1]
    (o0, l0), (o1, l1), (o2, l2) = oa_parts
    row = lambda width: pl.BlockSpec((tm, width), lambda i: (i, 0))
    resident = lambda rows: pl.BlockSpec((None, rows, d), lambda i: (layer, 0, 0),
                                         pipeline_mode=pl.Buffered(1))
    gate = lambda k: pl.BlockSpec((tm, d), lambda i: (i, k))
    modrow = lambda chunk: pl.BlockSpec((None, None, None, 1, d), _mod_spec(layer, chunk, seq_of))
    return pl.pallas_call(
        functools.partial(_merge_kernel, split=split),
        out_shape=(jax.ShapeDtypeStruct((n, d), F32), jax.ShapeDtypeStruct((n, d), BF16),
                   jax.ShapeDtypeStruct((n, ne), F32)),
        grid=(n // tm,),
        in_specs=[
            *_two_part_specs(tm, d, split), modrow(2),
            row(A_OUT), row(A_OUT), row(A_OUT), row(A_OUT), row(A_OUT), row(A_OUT),
            row(B_OUT), row(C_OUT),
            gate(0), gate(1), gate(2),
            resident(A_OUT), resident(B_OUT), resident(C_OUT), resident(d),
            pl.BlockSpec((None, 1, d), lambda i: (layer, 0, 0)), modrow(4), modrow(3),
            pl.BlockSpec((None, d, ne), lambda i: (layer, 0, 0)),
        ],
        out_specs=(row(d), row(d), row(ne)),
        compiler_params=_params(("arbitrary",), 56),
        name="merge",
    )(xa, xb, mod, o0, o1, o2, l0, l1, l2, ob, oc, mg, mg, mg, w_br_a, w_br_b, w_br_c, w_o,
      g2, mod, mod, w_router)


def _ffn_kernel(x_ref, gate_ref, wg_ref, wu_ref, wd_ref, o_ref, acc_scr):
    f = pl.program_id(2)

    @pl.when(f == 0)
    def _():
        acc_scr[...] = jnp.zeros_like(acc_scr)

    x = x_ref[...]
    a = jnp.dot(x, wg_ref[...], preferred_element_type=F32)
    u = jnp.dot(x, wu_ref[...], preferred_element_type=F32)
    hid = (a * jax.nn.sigmoid(a) * u).astype(BF16)
    acc_scr[...] += jnp.dot(hid, wd_ref[...], preferred_element_type=F32)

    @pl.when(f == pl.num_programs(2) - 1)
    def _():
        o_ref[...] = (acc_scr[...] * gate_ref[...]).astype(BF16)


def _expert_ffn(xe, gate, w_gate, w_up, w_down, layer, tm=1024, tf=512):
    ne, cap, d = xe.shape
    dff = w_gate.shape[-1]
    tm = min(tm, cap)
    return pl.pallas_call(
        _ffn_kernel,
        out_shape=jax.ShapeDtypeStruct((ne, cap, d), BF16),
        grid=(ne, cap // tm, dff // tf),
        in_specs=[
            pl.BlockSpec((None, tm, d), lambda e, i, f: (e, i, 0)),
            pl.BlockSpec((None, tm, 1), lambda e, i, f: (e, i, 0)),
            pl.BlockSpec((None, None, d, tf), lambda e, i, f: (layer, e, 0, f)),
            pl.BlockSpec((None, None, d, tf), lambda e, i, f: (layer, e, 0, f)),
            pl.BlockSpec((None, None, tf, d), lambda e, i, f: (layer, e, f, 0)),
        ],
        out_specs=pl.BlockSpec((None, tm, d), lambda e, i, f: (e, i, 0)),
        scratch_shapes=[pltpu.VMEM((tm, d), F32)],
        compiler_params=_params(("arbitrary", "arbitrary", "arbitrary"), 48),
        name="expert_ffn",
    )(xe, gate, w_gate, w_up, w_down)


WINDOW = 128
PAIR = 2 * WINDOW
N_PAIR_BUFS = 6


def _combine_copy(ye_hbm, buf, sem, expert, window, slot, half):
    start = pl.multiple_of(window * WINDOW, WINDOW)
    return pltpu.make_async_copy(ye_hbm.at[expert, pl.ds(start, WINDOW), :],
                                 buf.at[slot, pl.ds(half * WINDOW, WINDOW), :], sem.at[slot, half])


def _combine_kernel(cnt_ref, we_ref, ww_ref, idx_ref, x_ref, gt_ref, g_ref, sc_ref, sh_ref, ye_hbm,
                    o_ref, o2_ref, buf, sem, acc_scr, *, tm, final_norm, split):
    tile = pl.program_id(0)
    n = cnt_ref[tile]
    npairs = (n + 1) // 2
    tokens = tile * tm + lax.broadcasted_iota(jnp.int32, (tm, WINDOW), 0)
    acc_scr[...] = jnp.zeros_like(acc_scr)

    @pl.when(tile == 0)
    def _():
        buf[...] = jnp.zeros_like(buf)

    def copy(t, q, half):
        p = 2 * q + half
        return _combine_copy(ye_hbm, buf, sem, we_ref[t, p], ww_ref[t, p],
                             lax.rem(q, N_PAIR_BUFS), half)

    def start_pair(t, n_t, q):
        for half in range(2):
            @pl.when(2 * q + half < n_t)
            def _(half=half):
                copy(t, q, half).start()

    def start_first_pairs(t):
        n_t = cnt_ref[t]
        for ahead in range(N_PAIR_BUFS - 1):
            @pl.when(2 * ahead < n_t)
            def _(ahead=ahead):
                start_pair(t, n_t, ahead)

    @pl.when(tile == 0)
    def _():
        start_first_pairs(0)

    def body(q, _):
        for half in range(2):
            @pl.when(2 * q + half < n)
            def _(half=half):
                copy(tile, q, half).wait()

        @pl.when(q + N_PAIR_BUFS - 1 < npairs)
        def _():
            start_pair(tile, n, q + N_PAIR_BUFS - 1)

        blocks = []
        for half in range(2):
            p = 2 * q + half
            ids = idx_ref[we_ref[tile, p], ww_ref[tile, p]]
            blocks.append(jnp.where((ids == tokens) & (p < n), 1.0, 0.0).astype(BF16))
        onehot = jnp.concatenate(blocks, axis=1)
        acc_scr[...] += jnp.dot(onehot, buf[lax.rem(q, N_PAIR_BUFS)], preferred_element_type=F32)
        return 0

    lax.fori_loop(0, npairs, body, 0)

    @pl.when(tile + 1 < pl.num_programs(0))
    def _():
        start_first_pairs(tile + 1)

    x = x_ref[...] + gt_ref[...] * acc_scr[...]
    if final_norm:
        ms = jnp.mean(x * x, axis=-1, keepdims=True)
        y = x * lax.rsqrt(ms + RMS_EPS) * g_ref[...]

        @pl.when(tile < split)
        def _():
            o_ref[...] = y

        @pl.when(tile >= split)
        def _():
            o2_ref[...] = y
    else:
        o_ref[...] = x
        o2_ref[...] = _rms_mod(x, g_ref[...], sc_ref[...], sh_ref[...]).astype(BF16)


def _combine_plan(idx_sorted, n_tokens, tm):
    ne, cap = idx_sorted.shape
    nt = n_tokens // tm
    bounds = jnp.arange(nt + 1, dtype=jnp.int32) * tm
    base = jnp.sum(idx_sorted[:, :, None] < bounds[None, None, :], axis=1, dtype=jnp.int32)
    lo, hi = base[:, :-1], base[:, 1:]
    first = lo // WINDOW
    nwin = jnp.where(hi > lo, (hi - 1) // WINDOW - first + 1, 0)
    cum_incl = jnp.cumsum(nwin, axis=0)
    cum_excl = cum_incl - nwin
    count = cum_incl[-1]
    max_pairs = ne * (tm // WINDOW + 1) + 1
    p = jnp.arange(max_pairs, dtype=jnp.int32)
    expert = jnp.sum(cum_incl.T[:, None, :] <= p[None, :, None], axis=-1, dtype=jnp.int32)
    expert = jnp.minimum(expert, ne - 1)
    k = p[None, :] - jnp.take_along_axis(cum_excl.T, expert, axis=1)
    window = jnp.take_along_axis(first.T, expert, axis=1) + k
    window = jnp.clip(window, 0, cap // WINDOW - 1)
    return count.astype(jnp.int32), expert, window.astype(jnp.int32)


def _combine(x, ye, idx_sorted, mod, g_next, layer, seq_len, final_norm, n_first=0, tm=512):
    n, d = x.shape
    tm = min(tm, seq_len)
    tiles_per_seq = seq_len // tm
    seq_of = lambda i, *_: i // tiles_per_seq
    count, expert, window = _combine_plan(idx_sorted, n, tm)
    ne, cap = idx_sorted.shape
    idx_windows = idx_sorted.reshape(ne, cap // WINDOW, 1, WINDOW)
    split = n_first // tm
    kern = functools.partial(_combine_kernel, tm=tm, final_norm=final_norm, split=split)
    row = pl.BlockSpec((tm, d), lambda i, *_: (i, 0))
    modrow = lambda lyr, chunk: pl.BlockSpec((None, None, None, 1, d), _mod_spec(lyr, chunk, seq_of))
    if final_norm:
        g_spec = pl.BlockSpec((1, d), lambda i, *_: (0, 0))
        norm_specs = [g_spec, modrow(layer, 1), modrow(layer, 0)]
        out_shape = (jax.ShapeDtypeStruct((n_first, d), F32),
                     jax.ShapeDtypeStruct((n - n_first, d), F32))
        out_specs = _two_part_specs(tm, d, split)
    else:
        g_spec = pl.BlockSpec((None, 1, d), lambda i, *_: (layer + 1, 0, 0))
        norm_specs = [g_spec, modrow(layer + 1, 1), modrow(layer + 1, 0)]
        out_shape = (jax.ShapeDtypeStruct((n, d), F32), jax.ShapeDtypeStruct((n, d), BF16))
        out_specs = (row, row)
    grid_spec = pltpu.PrefetchScalarGridSpec(
        num_scalar_prefetch=3,
        grid=(n // tm,),
        in_specs=[
            pl.BlockSpec(idx_windows.shape, lambda i, *_: (0, 0, 0, 0)),
            row,
            modrow(layer, N_MOD - 1),
            *norm_specs,
            pl.BlockSpec(memory_space=pl.ANY),
        ],
        out_specs=out_specs,
        scratch_shapes=[pltpu.VMEM((N_PAIR_BUFS, PAIR, d), BF16),
                        pltpu.SemaphoreType.DMA((N_PAIR_BUFS, 2)),
                        pltpu.VMEM((tm, d), F32)],
    )
    return pl.pallas_call(
        kern,
        out_shape=out_shape,
        grid_spec=grid_spec,
        compiler_params=_params(("arbitrary",), 40),
        name="combine",
    )(count, expert, window, idx_windows, x, mod, g_next, mod, mod, ye)


def _trunk(xa, xb, c_rows, seq_len, g_norm1, g_norm2, w_ada, b_ada, w_in, w_branch_gate,
           w_br_a, w_br_b, w_br_c, w_o, lambda_q1, lambda_k1, lambda_q2, lambda_k2, g_subln, sink,
           w_router, w_e_gate, w_e_up, w_e_down, g_final):
    d = xa.shape[1]
    group_sizes = (xa.shape[0], xb.shape[0])
    n = sum(group_sizes)
    depth = w_in.shape[0]
    batch = n // seq_len
    s_a, s_b, s_c = _alibi_slopes()

    w_cat, col_scale = _proj_weights(w_in, w_branch_gate)
    wa, wb, wc, wo = (w.astype(BF16) for w in (w_br_a, w_br_b, w_br_c, w_o))
    weg, weu, wed = (w.astype(BF16) for w in (w_e_gate, w_e_up, w_e_down))
    row3 = lambda a: a.reshape(depth, 1, a.shape[-1])
    g1, g2 = row3(g_norm1), row3(g_norm2)
    lq1, lk1, lq2, lk2, gsub = (row3(a) for a in (lambda_q1, lambda_k1, lambda_q2, lambda_k2, g_subln))
    feat = _position_features(seq_len)

    mod = _ada_modulation(c_rows, w_ada, b_ada)
    mod = mod.reshape(depth, c_rows.shape[0], N_MOD, 1, d)

    h1 = _norm_mod(xa, xb, g1, mod, 0, seq_len)
    for layer in range(depth):
        *a_qkv, mg = _proj(h1, w_cat, col_scale, layer, batch, seq_len)
        oa_parts = [
            _dilated_attention(a_qkv[gi], gi,
                               s_a[gi * A_HEADS_PER_GROUP:(gi + 1) * A_HEADS_PER_GROUP],
                               batch, seq_len)
            for gi in range(len(A_GROUPS))
        ]
        ob = _diff_attention(mg, feat, jnp.asarray(s_b), lq1, lk1, lq2, lk2, gsub, layer, batch,
                             seq_len)
        oc = _window_attention(mg, jnp.asarray(s_c), sink[layer], batch, seq_len)
        x, h2, aff = _merge(xa, xb, mod, oa_parts, ob, oc, mg, wa, wb, wc, wo, g2, w_router, layer,
                            seq_len)
        aff_t = aff.T

        idx_parts, gate_parts = [], []
        start = 0
        for size in group_sizes:
            cap = (EC_CAPACITY_FACTOR * size) // N_EXPERTS
            gate, idx = lax.top_k(aff_t[:, start:start + size], cap)
            idx_parts.append(idx + start)
            gate_parts.append(gate)
            start += size
        idx = jnp.concatenate(idx_parts, axis=1)
        gate = jnp.concatenate(gate_parts, axis=1)
        idx, gate = lax.sort((idx, gate), dimension=1, num_keys=1)
        xe = jnp.take(h2, idx, axis=0, mode="clip")
        ye = _expert_ffn(xe, gate[..., None], weg, weu, wed, layer)
        if layer == depth - 1:
            return _combine(x, ye, idx, mod, g_final.reshape(1, d), layer, seq_len, True,
                            n_first=group_sizes[0])
        xa, h1 = _combine(x, ye, idx, mod, g1, layer, seq_len, False)
        xb = xa


def kernel(x_prompt, x_sample, c_prompt, c_sample, g_norm1, g_norm2, w_ada, b_ada, w_in, w_branch_gate, w_br_a, w_br_b, w_br_c, w_o, lambda_q1, lambda_k1, lambda_q2, lambda_k2, g_subln, sink, w_router, w_e_gate, w_e_up, w_e_down, g_final):
    bp, seq_len, d = x_prompt.shape
    bs = x_sample.shape[0]
    assert x_sample.shape[1] == seq_len
    c = jnp.concatenate([c_prompt, c_sample], axis=0)
    pad = -c.shape[0] % SUBLANES
    c_rows = jnp.pad(c, ((0, pad), (0, 0)))
    y_prompt, y_sample = _trunk(
        x_prompt.reshape(bp * seq_len, d), x_sample.reshape(bs * seq_len, d), c_rows, seq_len,
        g_norm1, g_norm2, w_ada, b_ada, w_in, w_branch_gate, w_br_a, w_br_b, w_br_c, w_o,
        lambda_q1, lambda_k1, lambda_q2, lambda_k2, g_subln, sink, w_router, w_e_gate, w_e_up,
        w_e_down, g_final)
    return (y_prompt.reshape(bp, seq_len, d), y_sample.reshape(bs, seq_len, d))
```
